```python
import math
import jax, jax.numpy as jnp
from jax import lax
import numpy as np

D_MODEL = 2048
BATCH = 2
SEQ = 16384
DEPTH = 2
DEC_BATCH = 4
DEC_SEQ = 4096
PAST_LEN = 128

MLA_HEADS = 6
MLA_Q_RANK = 512
MLA_KV_RANK = 256
MLA_NOPE = 128
MLA_ROPE = 64
MLA_V = 128
DIFF_HEADS = 6
DIFF_DK = 64
DIFF_DV = 2 * DIFF_DK
CONV_CH = 512
CONV_WIDTH = 31
MIX_WIDTH = MLA_HEADS * MLA_V + DIFF_HEADS * DIFF_DV + CONV_CH
IN_SIZES = (MLA_Q_RANK, MLA_KV_RANK, MLA_ROPE,
            DIFF_HEADS * 2 * DIFF_DK, DIFF_HEADS * 2 * DIFF_DK, DIFF_HEADS * DIFF_DV,
            2 * CONV_CH)
IN_COLS = sum(IN_SIZES)
D_FF = 5632
N_MOD = 9
ROPE_THETA = 10000.0
Q_BLOCK = 128
EPS = 1e-6

kernel_name = "hybrid_mla_diff_conformer_encoder"


def rms_norm(x, g):
    xf = x.astype(jnp.float32)
    y = xf * lax.rsqrt(jnp.mean(xf * xf, axis=-1, keepdims=True) + EPS)
    return (y * g.astype(jnp.float32)).astype(x.dtype)


def layer_norm(x, g, b):
    xf = x.astype(jnp.float32)
    mu = jnp.mean(xf, axis=-1, keepdims=True)
    var = jnp.mean(jnp.square(xf - mu), axis=-1, keepdims=True)
    y = (xf - mu) * lax.rsqrt(var + EPS)
    return (y * g.astype(jnp.float32) + b.astype(jnp.float32)).astype(x.dtype)


def rotary(x, pos):
    d = x.shape[-1]
    half = d // 2
    inv_freq = jnp.float32(ROPE_THETA) ** (-jnp.arange(half, dtype=jnp.float32) * (2.0 / d))
    ang = pos.astype(jnp.float32)[:, None] * inv_freq[None, :]
    bshape = (1, pos.shape[0]) + (1,) * (x.ndim - 3) + (half,)
    cos = jnp.cos(ang).reshape(bshape)
    sin = jnp.sin(ang).reshape(bshape)
    xf = x.astype(jnp.float32)
    x1, x2 = xf[..., :half], xf[..., half:]
    return jnp.concatenate([x1 * cos - x2 * sin, x1 * sin + x2 * cos], axis=-1).astype(x.dtype)


def block_attention(q, k, v, map_w, scale):
    B, S, H, M, dk = q.shape
    dv = v.shape[-1]
    nb = S // Q_BLOCK
    qb = jnp.moveaxis(q.reshape(B, nb, Q_BLOCK, H, M, dk), 1, 0)

    def one_block(qi):
        s = jnp.einsum('bqhmd,bkhmd->bhmqk', qi, k, preferred_element_type=jnp.float32) * scale
        p = jax.nn.softmax(s, axis=-1)
        p = jnp.einsum('bhmqk,m->bhqk', p, map_w)
        return jnp.einsum('bhqk,bkhd->bqhd', p.astype(v.dtype), v)

    o = lax.map(one_block, qb)
    return jnp.moveaxis(o, 0, 1).reshape(B, S, H, dv)


def swiglu(h, w_up, w_down):
    a, b = jnp.split(h @ w_up, 2, axis=-1)
    return (jax.nn.silu(a) * b) @ w_down


def token_mixer(h, pos, layer_idx, w_in, g_cq, g_ckv, w_uq, w_ukv, g_mla_q, g_mla_k,
                g_diff_q, g_diff_k, lam_q1, lam_k1, lam_q2, lam_k2, g_diff_sub,
                w_dw, b_dw, g_conv, b_conv, w_out):
    B, S, _ = h.shape
    split_at = [int(v) for v in np.cumsum(IN_SIZES)[:-1]]
    cq, ckv, k_rope, dq, dk, dv, cu = jnp.split(h @ w_in, split_at, axis=-1)

    q = (rms_norm(cq, g_cq) @ w_uq).reshape(B, S, MLA_HEADS, MLA_NOPE + MLA_ROPE)
    kv = (rms_norm(ckv, g_ckv) @ w_ukv).reshape(B, S, MLA_HEADS, MLA_NOPE + MLA_V)
    k_nope, v = kv[..., :MLA_NOPE], kv[..., MLA_NOPE:]
    k_r = jnp.broadcast_to(k_rope[:, :, None, :], (B, S, MLA_HEADS, MLA_ROPE))
    k = jnp.concatenate([k_nope, k_r], axis=-1)
    q = rms_norm(q, g_mla_q)
    k = rms_norm(k, g_mla_k)
    q = jnp.concatenate([q[..., :MLA_NOPE], rotary(q[..., MLA_NOPE:], pos)], axis=-1)
    k = jnp.concatenate([k[..., :MLA_NOPE], rotary(k[..., MLA_NOPE:], pos)], axis=-1)
    o_mla = block_attention(q[:, :, :, None], k[:, :, :, None], v,
                            jnp.ones((1,), jnp.float32), 1.0 / math.sqrt(MLA_NOPE + MLA_ROPE))

    lam_init = 0.8 - 0.6 * math.exp(-0.3 * layer_idx)
    lam = (jnp.exp(jnp.sum(lam_q1.astype(jnp.float32) * lam_k1.astype(jnp.float32)))
           - jnp.exp(jnp.sum(lam_q2.astype(jnp.float32) * lam_k2.astype(jnp.float32)))
           + lam_init)
    dq = rotary(rms_norm(dq.reshape(B, S, DIFF_HEADS, 2, DIFF_DK), g_diff_q), pos)
    dk = rotary(rms_norm(dk.reshape(B, S, DIFF_HEADS, 2, DIFF_DK), g_diff_k), pos)
    dv = dv.reshape(B, S, DIFF_HEADS, DIFF_DV)
    map_w = jnp.stack([jnp.ones((), jnp.float32), -lam])
    o_diff = block_attention(dq, dk, dv, map_w, 1.0 / math.sqrt(DIFF_DK))
    o_diff = rms_norm(o_diff, g_diff_sub) * (1.0 - lam_init)

    a, gt = jnp.split(cu, 2, axis=-1)
    glu = a * jax.nn.sigmoid(gt)
    conv = lax.conv_general_dilated(
        glu, w_dw[:, None, :].astype(glu.dtype), window_strides=(1,),
        padding=[(CONV_WIDTH // 2, CONV_WIDTH // 2)],
        dimension_numbers=('NWC', 'WIO', 'NWC'), feature_group_count=CONV_CH)
    conv = conv + b_dw.astype(conv.dtype)
    o_conv = jax.nn.silu(layer_norm(conv, g_conv, b_conv))

    mix = jnp.concatenate([o_mla.reshape(B, S, MLA_HEADS * MLA_V),
                           o_diff.reshape(B, S, DIFF_HEADS * DIFF_DV),
                           o_conv], axis=-1)
    return mix @ w_out


def encoder_layer(x, c, pos, layer_idx, w_ada, b_ada, g_norm, w_ffn1_in, w_ffn1_out,
                  w_ffn2_in, w_ffn2_out, w_in, g_cq, g_ckv, w_uq, w_ukv, g_mla_q, g_mla_k,
                  g_diff_q, g_diff_k, lam_q1, lam_k1, lam_q2, lam_k2, g_diff_sub,
                  w_dw, b_dw, g_conv, b_conv, w_out):
    B = x.shape[0]
    mod = (jax.nn.silu(c) @ w_ada + b_ada).reshape(B, N_MOD, D_MODEL)
    sh1, sc1, ga1, sh2, sc2, ga2, sh3, sc3, ga3 = [mod[:, i, None, :] for i in range(N_MOD)]

    h = rms_norm(x, g_norm[0]) * (1.0 + sc1) + sh1
    x = x + 0.5 * ga1 * swiglu(h, w_ffn1_in, w_ffn1_out)

    h = rms_norm(x, g_norm[1]) * (1.0 + sc2) + sh2
    x = x + ga2 * token_mixer(h, pos, layer_idx, w_in, g_cq, g_ckv, w_uq, w_ukv, g_mla_q, g_mla_k,
                              g_diff_q, g_diff_k, lam_q1, lam_k1, lam_q2, lam_k2, g_diff_sub,
                              w_dw, b_dw, g_conv, b_conv, w_out)

    h = rms_norm(x, g_norm[2]) * (1.0 + sc3) + sh3
    x = x + 0.5 * ga3 * swiglu(h, w_ffn2_in, w_ffn2_out)
    return rms_norm(x, g_norm[3])


def setup_inputs(seed: int = 0) -> dict:
    key = jax.random.key(seed)
    ks = jax.random.split(key, 32)
    f32 = jnp.float32

    def nrm(k, shape, scale):
        return jax.random.normal(k, shape, f32) * scale

    def gain(k, shape):
        return 1.0 + 0.05 * jax.random.normal(k, shape, f32)

    L = DEPTH
    return {
        "x_prompt": nrm(ks[0], (BATCH, SEQ, D_MODEL), 1.0),
        "x_sample": nrm(ks[1], (DEC_BATCH, DEC_SEQ, D_MODEL), 1.0),
        "c_prompt": nrm(ks[2], (BATCH, D_MODEL), 1.0),
        "c_sample": nrm(ks[3], (DEC_BATCH, D_MODEL), 1.0),
        "w_ada": nrm(ks[4], (L, D_MODEL, N_MOD * D_MODEL), 0.5 * D_MODEL ** -0.5),
        "b_ada": nrm(ks[5], (L, N_MOD * D_MODEL), 0.02),
        "g_norm": gain(ks[6], (L, 4, D_MODEL)),
        "w_ffn1_in": nrm(ks[7], (L, D_MODEL, 2 * D_FF), D_MODEL ** -0.5),
        "w_ffn1_out": nrm(ks[8], (L, D_FF, D_MODEL), D_FF ** -0.5),
        "w_ffn2_in": nrm(ks[9], (L, D_MODEL, 2 * D_FF), D_MODEL ** -0.5),
        "w_ffn2_out": nrm(ks[10], (L, D_FF, D_MODEL), D_FF ** -0.5),
        "w_in": nrm(ks[11], (L, D_MODEL, IN_COLS), D_MODEL ** -0.5),
        "g_cq": gain(ks[12], (L, MLA_Q_RANK)),
        "g_ckv": gain(ks[13], (L, MLA_KV_RANK)),
        "w_uq": nrm(ks[14], (L, MLA_Q_RANK, MLA_HEADS * (MLA_NOPE + MLA_ROPE)), MLA_Q_RANK ** -0.5),
        "w_ukv": nrm(ks[15], (L, MLA_KV_RANK, MLA_HEADS * (MLA_NOPE + MLA_V)), MLA_KV_RANK ** -0.5),
        "g_mla_q": gain(ks[16], (L, MLA_NOPE + MLA_ROPE)),
        "g_mla_k": gain(ks[17], (L, MLA_NOPE + MLA_ROPE)),
        "g_diff_q": gain(ks[18], (L, DIFF_DK)),
        "g_diff_k": gain(ks[19], (L, DIFF_DK)),
        "lam_q1": nrm(ks[20], (L, DIFF_DK), 0.1),
        "lam_k1": nrm(ks[21], (L, DIFF_DK), 0.1),
        "lam_q2": nrm(ks[22], (L, DIFF_DK), 0.1),
        "lam_k2": nrm(ks[23], (L, DIFF_DK), 0.1),
        "g_diff_sub": gain(ks[24], (L, DIFF_DV)),
        "w_dw": nrm(ks[25], (L, CONV_WIDTH, CONV_CH), CONV_WIDTH ** -0.5),
        "b_dw": nrm(ks[26], (L, CONV_CH), 0.02),
        "g_conv": gain(ks[27], (L, CONV_CH)),
        "b_conv": nrm(ks[28], (L, CONV_CH), 0.02),
        "w_out": nrm(ks[29], (L, MIX_WIDTH, D_MODEL), MIX_WIDTH ** -0.5),
    }


def reference(x_prompt, x_sample, c_prompt, c_sample, w_ada, b_ada, g_norm, w_ffn1_in, w_ffn1_out,
              w_ffn2_in, w_ffn2_out, w_in, g_cq, g_ckv, w_uq, w_ukv, g_mla_q, g_mla_k,
              g_diff_q, g_diff_k, lam_q1, lam_k1, lam_q2, lam_k2, g_diff_sub,
              w_dw, b_dw, g_conv, b_conv, w_out):
    def trunk(x, c):
        pos = jnp.arange(x.shape[1], dtype=jnp.int32)
        for l in range(DEPTH):
            x = encoder_layer(x, c, pos, l, w_ada[l], b_ada[l], g_norm[l], w_ffn1_in[l], w_ffn1_out[l],
                              w_ffn2_in[l], w_ffn2_out[l], w_in[l], g_cq[l], g_ckv[l], w_uq[l], w_ukv[l],
                              g_mla_q[l], g_mla_k[l], g_diff_q[l], g_diff_k[l],
                              lam_q1[l], lam_k1[l], lam_q2[l], lam_k2[l], g_diff_sub[l],
                              w_dw[l], b_dw[l], g_conv[l], b_conv[l], w_out[l])
        return x

    y_prompt = trunk(x_prompt, c_prompt)
    y_sample = trunk(x_sample, c_sample)
    return (y_prompt, y_sample)
```

```python
import functools
import math

import numpy as np
import jax
import jax.numpy as jnp
from jax import lax
from jax.experimental import pallas as pl
from jax.experimental.pallas import tpu as pltpu

F32 = jnp.float32
BF16 = jnp.bfloat16

D_MODEL = 2048
MLA_HEADS = 6
MLA_Q_RANK = 512
MLA_KV_RANK = 256
MLA_NOPE = 128
MLA_ROPE = 64
MLA_V = 128
MLA_QK = MLA_NOPE + MLA_ROPE
MLA_QK_PAD = 256
DIFF_HEADS = 6
DIFF_DK = 64
DIFF_DV = 128
CONV_CH = 512
CONV_WIDTH = 31
CONV_HALO = 16
D_FF = 5632
N_MOD = 9
ROPE_THETA = 10000.0
EPS = 1e-6
LANE = 128

SEC_A = (0, 896)
SEC_DQ = (896, 1664)
SEC_DK = (1664, 2432)
SEC_DV = (2432, 3200)
SEC_CU = (3200, 4224)
IN_COLS_PAD = 4224

VMEM_LIMIT = 56 * 1024 * 1024


def _dot(a, b):
    return jnp.dot(a, b, preferred_element_type=F32)


def _dot_nt(a, b):
    return lax.dot_general(a, b, (((1,), (1,)), ((), ())), preferred_element_type=F32)


def _rms(x, width):
    return x * lax.rsqrt(jnp.sum(x * x, axis=-1, keepdims=True) * (1.0 / width) + EPS)


def _adaln(x, g, sc, sh):
    return (_rms(x, x.shape[-1]) * g) * (1.0 + sc) + sh


def _rope(v, cos, sin_lo, sin_hi):
    return v * cos + pltpu.roll(v, 96, 1) * sin_lo + pltpu.roll(v, 32, 1) * sin_hi


def _params(sem, vmem=VMEM_LIMIT):
    return pltpu.CompilerParams(dimension_semantics=sem, vmem_limit_bytes=vmem)


def _ada_kernel(c_ref, w_ref, b_ref, o_ref):
    c = c_ref[...]
    s = c * jax.nn.sigmoid(c)
    s_hi = s.astype(BF16)
    s_lo = (s - s_hi.astype(F32)).astype(BF16)
    w = w_ref[0]
    w_hi = w.astype(BF16)
    w_lo = (w - w_hi.astype(F32)).astype(BF16)
    o_ref[0] = _dot(s_hi, w_hi) + _dot(s_lo, w_hi) + _dot(s_hi, w_lo) + b_ref[0]


def _ada_call(c_all, w_ada, b_ada, tn=512):
    L, D, N = w_ada.shape
    R = c_all.shape[0]
    return pl.pallas_call(
        _ada_kernel,
        grid=(L, N // tn),
        in_specs=[
            pl.BlockSpec((R, D), lambda l, n: (0, 0)),
            pl.BlockSpec((1, D, tn), lambda l, n: (l, 0, n)),
            pl.BlockSpec((1, 1, tn), lambda l, n: (l, 0, n)),
        ],
        out_specs=pl.BlockSpec((1, R, tn), lambda l, n: (l, 0, n)),
        out_shape=jax.ShapeDtypeStruct((L, R, N), F32),
        compiler_params=_params(("arbitrary", "arbitrary")),
        name="ada",
    )(c_all, w_ada, b_ada.reshape(L, 1, N))


def _ffn_kernel(x_ref, mod_ref, g_ref, wa_ref, wb_ref, wd_ref, gf_ref, o_ref, h_ref, *, sub, final_norm):
    j = pl.program_id(2)
    nj = pl.num_programs(2)

    @pl.when(j == 0)
    def _():
        sh = mod_ref[0, 3 * sub:3 * sub + 1, :]
        sc = mod_ref[0, 3 * sub + 1:3 * sub + 2, :]
        h_ref[...] = _adaln(x_ref[0], g_ref[...], sc, sh).astype(BF16)

    h = h_ref[...]
    a = _dot(h, wa_ref[...])
    b = _dot(h, wb_ref[...])
    g = (a * jax.nn.sigmoid(a) * b).astype(BF16)
    contrib = _dot(g, wd_ref[...])

    @pl.when(j == 0)
    def _():
        o_ref[0] = contrib

    @pl.when(j > 0)
    def _():
        o_ref[0] += contrib

    @pl.when(j == nj - 1)
    def _():
        ga = mod_ref[0, 3 * sub + 2:3 * sub + 3, :]
        y = x_ref[0] + 0.5 * ga * o_ref[0]
        if final_norm:
            y = _rms(y, y.shape[-1]) * gf_ref[...]
        o_ref[0] = y


def _ffn_call(x, mod, g_pre, w_up, w_down, g_fin, *, sub, final_norm, tm, tf):
    B, S, D = x.shape
    nf = D_FF // tf
    kern = functools.partial(_ffn_kernel, sub=sub, final_norm=final_norm)
    return pl.pallas_call(
        kern,
        grid=(B, S // tm, nf),
        in_specs=[
            pl.BlockSpec((1, tm, D), lambda b, i, j: (b, i, 0)),
            pl.BlockSpec((1, N_MOD, D), lambda b, i, j: (b, 0, 0)),
            pl.BlockSpec((1, D), lambda b, i, j: (0, 0)),
            pl.BlockSpec((D, tf), lambda b, i, j: (0, j)),
            pl.BlockSpec((D, tf), lambda b, i, j: (0, j + nf)),
            pl.BlockSpec((tf, D), lambda b, i, j: (j, 0)),
            pl.BlockSpec((1, D), lambda b, i, j: (0, 0)),
        ],
        out_specs=pl.BlockSpec((1, tm, D), lambda b, i, j: (b, i, 0)),
        out_shape=jax.ShapeDtypeStruct((B, S, D), F32),
        scratch_shapes=[pltpu.VMEM((tm, D), BF16)],
        compiler_params=_params(("arbitrary", "arbitrary", "arbitrary")),
        name="ffn",
    )(x, mod, g_pre, w_up, w_up, w_down, g_fin)


def _proj_kernel(x_ref, mod_ref, g_ref, win_ref, gcq_ref, gckv_ref, wuq_ref, wukv_ref, gq_ref, gk_ref,
                 gdq_ref, gdk_ref, bd_ref, cos_ref, slo_ref, shi_ref,
                 q_ref, k_ref, v_ref, dq_ref, dk_ref, dv_ref, glu_ref):
    sh = mod_ref[0, 3:4, :]
    sc = mod_ref[0, 4:5, :]
    h = _adaln(x_ref[0], g_ref[...], sc, sh).astype(BF16)
    cos = cos_ref[...]
    slo = slo_ref[...]
    shi = shi_ref[...]

    pa = _dot(h, win_ref[:, SEC_A[0]:SEC_A[1]])
    cq = pa[:, 0:MLA_Q_RANK]
    ckv = pa[:, MLA_Q_RANK:MLA_Q_RANK + MLA_KV_RANK]
    kr = pa[:, MLA_Q_RANK + MLA_KV_RANK:SEC_A[1]]
    cqn = (_rms(cq, MLA_Q_RANK) * gcq_ref[...]).astype(BF16)
    ckvn = (_rms(ckv, MLA_KV_RANK) * gckv_ref[...]).astype(BF16)
    qraw = _dot(cqn, wuq_ref[...])
    kv = _dot(ckvn, wukv_ref[...])
    kr_ss = jnp.sum(kr * kr, axis=-1, keepdims=True)
    gq = gq_ref[...]
    gk = gk_ref[...]
    for hd in range(MLA_HEADS):
        c0 = hd * MLA_QK_PAD
        qh = qraw[:, c0:c0 + MLA_QK_PAD]
        qn = _rms(qh, MLA_QK) * gq
        q_ref[0, hd, :, 0:LANE] = qn[:, 0:LANE].astype(BF16)
        q_ref[0, hd, :, LANE:2 * LANE] = _rope(qn[:, LANE:2 * LANE], cos, slo, shi).astype(BF16)
        kn = kv[:, c0:c0 + LANE]
        inv = lax.rsqrt((jnp.sum(kn * kn, axis=-1, keepdims=True) + kr_ss) * (1.0 / MLA_QK) + EPS)
        k_ref[0, hd, :, 0:LANE] = (kn * inv * gk[:, 0:LANE]).astype(BF16)
        k_ref[0, hd, :, LANE:2 * LANE] = _rope(kr * inv * gk[:, LANE:2 * LANE], cos, slo, shi).astype(BF16)
        v_ref[0, hd] = kv[:, c0 + LANE:c0 + 2 * LANE].astype(BF16)

    bd = bd_ref[...]
    for sec, g_t, o_ref in ((SEC_DQ, gdq_ref, dq_ref), (SEC_DK, gdk_ref, dk_ref)):
        z = _dot(h, win_ref[:, sec[0]:sec[1]])
        zz = z * z
        zz_hi = zz.astype(BF16)
        zz_lo = (zz - zz_hi.astype(F32)).astype(BF16)
        ss = _dot(zz_hi, bd) + _dot(zz_lo, bd)
        zn = z * lax.rsqrt(ss * (1.0 / DIFF_DK) + EPS) * g_t[...]
        for hd in range(DIFF_HEADS):
            o_ref[0, hd] = _rope(zn[:, hd * LANE:(hd + 1) * LANE], cos, slo, shi).astype(BF16)
    z = _dot(h, win_ref[:, SEC_DV[0]:SEC_DV[1]])
    for hd in range(DIFF_HEADS):
        dv_ref[0, hd] = z[:, hd * LANE:(hd + 1) * LANE].astype(BF16)

    cu = _dot(h, win_ref[:, SEC_CU[0]:SEC_CU[1]])
    glu_ref[0] = cu[:, 0:CONV_CH] * jax.nn.sigmoid(cu[:, CONV_CH:2 * CONV_CH])


def _proj_call(x, mod, g_pre, lw, rope_tabs, *, tm):
    B, S, D = x.shape
    H = MLA_HEADS
    const = lambda shape: pl.BlockSpec(shape, lambda b, i: (0,) * len(shape))
    tab = pl.BlockSpec((tm, LANE), lambda b, i: (i, 0))
    head_out = lambda w: pl.BlockSpec((1, H, tm, w), lambda b, i: (b, 0, i, 0))
    head_shape = lambda w: jax.ShapeDtypeStruct((B, H, S, w), BF16)
    return pl.pallas_call(
        _proj_kernel,
        grid=(B, S // tm),
        in_specs=[
            pl.BlockSpec((1, tm, D), lambda b, i: (b, i, 0)),
            pl.BlockSpec((1, N_MOD, D), lambda b, i: (b, 0, 0)),
            const((1, D)),
            const((D, IN_COLS_PAD)),
            const((1, MLA_Q_RANK)),
            const((1, MLA_KV_RANK)),
            const((MLA_Q_RANK, H * MLA_QK_PAD)),
            const((MLA_KV_RANK, H * MLA_QK_PAD)),
            const((1, MLA_QK_PAD)),
            const((1, MLA_QK_PAD)),
            const((1, DIFF_HEADS * LANE)),
            const((1, DIFF_HEADS * LANE)),
            const((DIFF_HEADS * LANE, DIFF_HEADS * LANE)),
            tab, tab, tab,
        ],
        out_specs=[head_out(MLA_QK_PAD), head_out(MLA_QK_PAD), head_out(LANE),
                   head_out(LANE), head_out(LANE), head_out(LANE),
                   pl.BlockSpec((1, tm, CONV_CH), lambda b, i: (b, i, 0))],
        out_shape=[head_shape(MLA_QK_PAD), head_shape(MLA_QK_PAD), head_shape(LANE),
                   head_shape(LANE), head_shape(LANE), head_shape(LANE),
                   jax.ShapeDtypeStruct((B, S, CONV_CH), F32)],
        compiler_params=_params(("arbitrary", "arbitrary")),
        name="proj",
    )(x, mod, g_pre, lw["w_in"], lw["g_cq"], lw["g_ckv"], lw["w_uq"], lw["w_ukv"], lw["g_q"], lw["g_k"],
      lw["g_dq"], lw["g_dk"], lw["bd"], *rope_tabs)


def _softmax_step(s, vs, m_ref, l_ref, acc_ref):
    m_prev = m_ref[...]
    m_new = jnp.maximum(m_prev, jnp.max(s, axis=-1, keepdims=True))
    alpha = jnp.exp(m_prev - m_new)
    p = jnp.exp(s - m_new)
    l_ref[...] = alpha * l_ref[...] + jnp.sum(p, axis=-1, keepdims=True)
    acc_ref[...] = alpha * acc_ref[...] + _dot(p.astype(BF16), vs)
    m_ref[...] = m_new


def _mla_attn_kernel(q_ref, k_ref, v_ref, o_ref, m_ref, l_ref, acc_ref, *, tk):
    S = k_ref.shape[2]
    q = q_ref[0, 0]
    m_ref[...] = jnp.full(m_ref.shape, -jnp.inf, F32)
    l_ref[...] = jnp.zeros(l_ref.shape, F32)
    acc_ref[...] = jnp.zeros(acc_ref.shape, F32)

    def body(c, carry):
        r0 = pl.multiple_of(c * tk, tk)
        s = _dot_nt(q, k_ref[0, 0, pl.ds(r0, tk), :])
        _softmax_step(s, v_ref[0, 0, pl.ds(r0, tk), :], m_ref, l_ref, acc_ref)
        return carry

    lax.fori_loop(0, S // tk, body, 0)
    o_ref[0] = (acc_ref[...] / l_ref[...]).astype(o_ref.dtype)


def _diff_attn_kernel(q_ref, k_ref, v_ref, lam_ref, gsub_ref, o_ref,
                      m1_ref, l1_ref, a1_ref, m2_ref, l2_ref, a2_ref, *, tk, lam_init):
    S = k_ref.shape[2]
    q = q_ref[0, 0]
    lane = lax.broadcasted_iota(jnp.int32, q.shape, 1)
    zero = jnp.zeros_like(q)
    q1 = jnp.where(lane < DIFF_DK, q, zero)
    q2 = jnp.where(lane >= DIFF_DK, q, zero)
    for m_ref, l_ref, a_ref in ((m1_ref, l1_ref, a1_ref), (m2_ref, l2_ref, a2_ref)):
        m_ref[...] = jnp.full(m_ref.shape, -jnp.inf, F32)
        l_ref[...] = jnp.zeros(l_ref.shape, F32)
        a_ref[...] = jnp.zeros(a_ref.shape, F32)

    def body(c, carry):
        r0 = pl.multiple_of(c * tk, tk)
        ks = k_ref[0, 0, pl.ds(r0, tk), :]
        vs = v_ref[0, 0, pl.ds(r0, tk), :]
        _softmax_step(_dot_nt(q1, ks), vs, m1_ref, l1_ref, a1_ref)
        _softmax_step(_dot_nt(q2, ks), vs, m2_ref, l2_ref, a2_ref)
        return carry

    lax.fori_loop(0, S // tk, body, 0)
    lv = lam_ref[...]
    lam = (jnp.exp(jnp.sum(lv[0:1] * lv[1:2], axis=-1, keepdims=True))
           - jnp.exp(jnp.sum(lv[2:3] * lv[3:4], axis=-1, keepdims=True)) + lam_init)
    o = a1_ref[...] / l1_ref[...] - lam * (a2_ref[...] / l2_ref[...])
    o = _rms(o, DIFF_DV) * gsub_ref[...] * (1.0 - lam_init)
    o_ref[0] = o.astype(o_ref.dtype)


def _attn_call(q, k, v, extra, *, tq, tk, lam_init=None):
    B, H, S, dqk = q.shape
    dv = v.shape[-1]
    stat = pltpu.VMEM((tq, 1), F32)
    acc = pltpu.VMEM((tq, dv), F32)
    in_specs = [
        pl.BlockSpec((1, 1, tq, dqk), lambda b, h, i: (b, h, i, 0)),
        pl.BlockSpec((1, 1, S, dqk), lambda b, h, i: (b, h, 0, 0)),
        pl.BlockSpec((1, 1, S, dv), lambda b, h, i: (b, h, 0, 0)),
    ]
    if lam_init is None:
        kern = functools.partial(_mla_attn_kernel, tk=tk)
        scratch = [stat, stat, acc]
        name = "mla_attn"
    else:
        kern = functools.partial(_diff_attn_kernel, tk=tk, lam_init=lam_init)
        scratch = [stat, stat, acc, stat, stat, acc]
        in_specs += [pl.BlockSpec(e.shape, lambda b, h, i: (0, 0)) for e in extra]
        name = "diff_attn"
    return pl.pallas_call(
        kern,
        grid=(B, H, S // tq),
        in_specs=in_specs,
        out_specs=pl.BlockSpec((1, tq, dv), lambda b, h, i: (b, i, h)),
        out_shape=jax.ShapeDtypeStruct((B, S, H * dv), BF16),
        scratch_shapes=scratch,
        compiler_params=_params(("arbitrary", "arbitrary", "arbitrary")),
        name=name,
    )(q, k, v, *extra)


def _conv_kernel(prev_ref, cur_ref, next_ref, w_ref, b_ref, g_ref, beta_ref, o_ref, ext_ref, *, rows):
    i = pl.program_id(1)
    ni = pl.num_programs(1)
    tm = cur_ref.shape[1]
    halo = CONV_HALO
    prev = prev_ref[0]
    nxt = next_ref[0]
    ext_ref[0:halo, :] = jnp.where(i > 0, prev, jnp.zeros_like(prev))
    ext_ref[halo:halo + tm, :] = cur_ref[0]
    ext_ref[halo + tm:2 * halo + tm, :] = jnp.where(i < ni - 1, nxt, jnp.zeros_like(nxt))
    w = w_ref[...]
    bias = b_ref[...]
    g = g_ref[...]
    beta = beta_ref[...]
    off = halo - CONV_WIDTH // 2

    for r0 in range(0, tm, rows):
        acc = jnp.zeros((rows, CONV_CH), F32)
        for j in range(CONV_WIDTH):
            acc = acc + ext_ref[r0 + j + off:r0 + j + off + rows, :] * w[j:j + 1, :]
        acc = acc + bias
        mu = jnp.mean(acc, axis=-1, keepdims=True)
        d = acc - mu
        var = jnp.mean(d * d, axis=-1, keepdims=True)
        y = d * lax.rsqrt(var + EPS) * g + beta
        o_ref[0, r0:r0 + rows, :] = (y * jax.nn.sigmoid(y)).astype(o_ref.dtype)


def _conv_call(glu, w, b, g, beta, *, tm, rows=64):
    B, S, C = glu.shape
    hb = tm // CONV_HALO
    nh = S // CONV_HALO
    const = lambda shape: pl.BlockSpec(shape, lambda b_, i: (0, 0))
    return pl.pallas_call(
        functools.partial(_conv_kernel, rows=min(rows, tm)),
        grid=(B, S // tm),
        in_specs=[
            pl.BlockSpec((1, CONV_HALO, C), lambda b_, i: (b_, jnp.maximum(i * hb - 1, 0), 0)),
            pl.BlockSpec((1, tm, C), lambda b_, i: (b_, i, 0)),
            pl.BlockSpec((1, CONV_HALO, C), lambda b_, i: (b_, jnp.minimum((i + 1) * hb, nh - 1), 0)),
            const(w.shape), const((1, C)), const((1, C)), const((1, C)),
        ],
        out_specs=pl.BlockSpec((1, tm, C), lambda b_, i: (b_, i, 0)),
        out_shape=jax.ShapeDtypeStruct((B, S, C), BF16),
        scratch_shapes=[pltpu.VMEM((tm + 2 * CONV_HALO, C), F32)],
        compiler_params=_params(("arbitrary", "arbitrary")),
        name="conv",
    )(glu, glu, glu, w, b, g, beta)


def _out_kernel(x_ref, mod_ref, mla_ref, dif_ref, cnv_ref, w_ref, o_ref):
    n1 = mla_ref.shape[-1]
    n2 = n1 + dif_ref.shape[-1]
    acc = (_dot(mla_ref[0], w_ref[0:n1, :]) + _dot(dif_ref[0], w_ref[n1:n2, :])
           + _dot(cnv_ref[0], w_ref[n2:, :]))
    o_ref[0] = x_ref[0] + mod_ref[0, 5:6, :] * acc


def _out_call(x, mod, o_mla, o_dif, o_cnv, w_out, *, tm):
    B, S, D = x.shape
    tok = lambda w: pl.BlockSpec((1, tm, w), lambda b, i: (b, i, 0))
    return pl.pallas_call(
        _out_kernel,
        grid=(B, S // tm),
        in_specs=[tok(D), pl.BlockSpec((1, N_MOD, D), lambda b, i: (b, 0, 0)),
                  tok(o_mla.shape[-1]), tok(o_dif.shape[-1]), tok(o_cnv.shape[-1]),
                  pl.BlockSpec(w_out.shape, lambda b, i: (0, 0))],
        out_specs=tok(D),
        out_shape=jax.ShapeDtypeStruct((B, S, D), F32),
        compiler_params=_params(("arbitrary", "arbitrary")),
        name="out",
    )(x, mod, o_mla, o_dif, o_cnv, w_out)


def _rope_tables(S):
    half = DIFF_DK // 2
    inv_freq = jnp.float32(ROPE_THETA) ** (-jnp.arange(half, dtype=F32) * (2.0 / DIFF_DK))
    ang = jnp.arange(S, dtype=jnp.int32).astype(F32)[:, None] * inv_freq[None, :]
    c, s, z = jnp.cos(ang), jnp.sin(ang), jnp.zeros_like(ang)
    return (jnp.concatenate([c, c, c, c], axis=1),
            jnp.concatenate([-s, z, -s, z], axis=1),
            jnp.concatenate([z, s, z, s], axis=1))


def _layer_weights(l, w_in, g_cq, g_ckv, w_uq, w_ukv, g_mla_q, g_mla_k, g_diff_q, g_diff_k,
                   lam_q1, lam_k1, lam_q2, lam_k2, g_diff_sub, w_dw, b_dw, g_conv, b_conv, w_out):
    wi = w_in[l]
    o_rope = MLA_Q_RANK + MLA_KV_RANK
    o_dq = o_rope + MLA_ROPE
    w_in_p = jnp.concatenate(
        [wi[:, :o_dq], jnp.zeros((D_MODEL, LANE - MLA_ROPE), wi.dtype), wi[:, o_dq:]], axis=1).astype(BF16)
    w_uq_p = jnp.pad(w_uq[l].reshape(MLA_Q_RANK, MLA_HEADS, MLA_QK),
                     ((0, 0), (0, 0), (0, MLA_QK_PAD - MLA_QK))).reshape(MLA_Q_RANK, -1).astype(BF16)
    pad_qk = lambda g: jnp.pad(g, (0, MLA_QK_PAD - MLA_QK)).reshape(1, MLA_QK_PAD)
    return dict(
        w_in=w_in_p, g_cq=g_cq[l].reshape(1, -1), g_ckv=g_ckv[l].reshape(1, -1),
        w_uq=w_uq_p, w_ukv=w_ukv[l].astype(BF16),
        g_q=pad_qk(g_mla_q[l] * (1.0 / math.sqrt(MLA_QK))), g_k=pad_qk(g_mla_k[l]),
        g_dq=jnp.tile(g_diff_q[l] * (1.0 / math.sqrt(DIFF_DK)), 2 * DIFF_HEADS).reshape(1, -1),
        g_dk=jnp.tile(g_diff_k[l], 2 * DIFF_HEADS).reshape(1, -1),
        bd=jnp.asarray(np.kron(np.eye(2 * DIFF_HEADS), np.ones((DIFF_DK, DIFF_DK))), BF16),
        lam=jnp.stack([lam_q1[l], lam_k1[l], lam_q2[l], lam_k2[l]]),
        g_sub=g_diff_sub[l].reshape(1, -1),
        w_dw=jnp.pad(w_dw[l], ((0, 1), (0, 0))), b_dw=b_dw[l].reshape(1, -1),
        g_conv=g_conv[l].reshape(1, -1), b_conv=b_conv[l].reshape(1, -1),
        w_out=w_out[l].astype(BF16),
    )


def _tile(S, t):
    return min(S, t)


def kernel(x_prompt, x_sample, c_prompt, c_sample, w_ada, b_ada, g_norm, w_ffn1_in, w_ffn1_out, w_ffn2_in, w_ffn2_out, w_in, g_cq, g_ckv, w_uq, w_ukv, g_mla_q, g_mla_k, g_diff_q, g_diff_k, lam_q1, lam_k1, lam_q2, lam_k2, g_diff_sub, w_dw, b_dw, g_conv, b_conv, w_out):
    L = w_ada.shape[0]
    bp, bs = c_prompt.shape[0], c_sample.shape[0]
    rows = -(-(bp + bs) // 8) * 8
    c_all = jnp.pad(jnp.concatenate([c_prompt, c_sample], axis=0), ((0, rows - bp - bs), (0, 0)))
    mod_all = _ada_call(c_all, w_ada, b_ada)

    up1, dn1 = w_ffn1_in.astype(BF16), w_ffn1_out.astype(BF16)
    up2, dn2 = w_ffn2_in.astype(BF16), w_ffn2_out.astype(BF16)
    layers = [_layer_weights(l, w_in, g_cq, g_ckv, w_uq, w_ukv, g_mla_q, g_mla_k, g_diff_q, g_diff_k,
                             lam_q1, lam_k1, lam_q2, lam_k2, g_diff_sub, w_dw, b_dw, g_conv, b_conv, w_out)
              for l in range(L)]

    def trunk(x, row0):
        B, S, D = x.shape
        tm = _tile(S, 512)
        tabs = _rope_tables(S)
        for l in range(L):
            lw = layers[l]
            mod = mod_all[l, row0:row0 + B].reshape(B, N_MOD, D)
            gn = g_norm[l]
            x = _ffn_call(x, mod, gn[0:1], up1[l], dn1[l], gn[3:4], sub=0, final_norm=False, tm=tm, tf=512)
            q, k, v, dq, dk, dv, glu = _proj_call(x, mod, gn[1:2], lw, tabs, tm=tm)
            o_mla = _attn_call(q, k, v, (), tq=tm, tk=_tile(S, 512))
            lam_init = 0.8 - 0.6 * math.exp(-0.3 * l)
            o_dif = _attn_call(dq, dk, dv, (lw["lam"], lw["g_sub"]), tq=tm, tk=_tile(S, 512), lam_init=lam_init)
            o_cnv = _conv_call(glu, lw["w_dw"], lw["b_dw"], lw["g_conv"], lw["b_conv"], tm=tm)
            x = _out_call(x, mod, o_mla, o_dif, o_cnv, lw["w_out"], tm=tm)
            x = _ffn_call(x, mod, gn[2:3], up2[l], dn2[l], gn[3:4], sub=2, final_norm=True, tm=tm, tf=512)
        return x

    return (trunk(x_prompt, 0), trunk(x_sample, bp))
```

```python
import functools
import math

import numpy as np
import jax
import jax.numpy as jnp
from jax import lax
from jax.experimental import pallas as pl
from jax.experimental.pallas import tpu as pltpu

F32 = jnp.float32
BF16 = jnp.bfloat16

D_MODEL = 2048
MLA_HEADS = 6
MLA_Q_RANK = 512
MLA_KV_RANK = 256
MLA_NOPE = 128
MLA_ROPE = 64
MLA_V = 128
MLA_QK = MLA_NOPE + MLA_ROPE
MLA_QK_PAD = 256
DIFF_HEADS = 6
DIFF_DK = 64
DIFF_DV = 128
CONV_CH = 512
CONV_WIDTH = 31
CONV_HALO = 16
D_FF = 5632
N_MOD = 9
ROPE_THETA = 10000.0
EPS = 1e-6
LANE = 128

SEC_A = (0, 896)
SEC_DQ = (896, 1664)
SEC_DK = (1664, 2432)
SEC_DV = (2432, 3200)
SEC_CU = (3200, 4224)
IN_COLS_PAD = 4224

VMEM_LIMIT = 56 * 1024 * 1024
MAX_SHIFT_BOUND = 40.0


def _dot(a, b):
    return jnp.dot(a, b, preferred_element_type=F32)


def _dot_nt(a, b):
    return lax.dot_general(a, b, (((1,), (1,)), ((), ())), preferred_element_type=F32)


def _rms(x, width):
    return x * lax.rsqrt(jnp.sum(x * x, axis=-1, keepdims=True) * (1.0 / width) + EPS)


def _adaln(x, g, sc, sh):
    return (_rms(x, x.shape[-1]) * g) * (1.0 + sc) + sh


def _rope(v, cos, sin_lo, sin_hi):
    return v * cos + pltpu.roll(v, 96, 1) * sin_lo + pltpu.roll(v, 32, 1) * sin_hi


def _params(sem, vmem=VMEM_LIMIT):
    return pltpu.CompilerParams(dimension_semantics=sem, vmem_limit_bytes=vmem)


def _ada_kernel(c_ref, w_ref, b_ref, o_ref):
    c = c_ref[...]
    s = c * jax.nn.sigmoid(c)
    s_hi = s.astype(BF16)
    s_lo = (s - s_hi.astype(F32)).astype(BF16)
    w = w_ref[0]
    w_hi = w.astype(BF16)
    w_lo = (w - w_hi.astype(F32)).astype(BF16)
    o_ref[0] = _dot(s_hi, w_hi) + _dot(s_lo, w_hi) + _dot(s_hi, w_lo) + b_ref[0]


def _ada_call(c_all, w_ada, b_ada, tn=512):
    L, D, N = w_ada.shape
    R = c_all.shape[0]
    return pl.pallas_call(
        _ada_kernel,
        grid=(L, N // tn),
        in_specs=[
            pl.BlockSpec((R, D), lambda l, n: (0, 0)),
            pl.BlockSpec((1, D, tn), lambda l, n: (l, 0, n)),
            pl.BlockSpec((1, 1, tn), lambda l, n: (l, 0, n)),
        ],
        out_specs=pl.BlockSpec((1, R, tn), lambda l, n: (l, 0, n)),
        out_shape=jax.ShapeDtypeStruct((L, R, N), F32),
        compiler_params=_params(("arbitrary", "arbitrary")),
        name="ada",
    )(c_all, w_ada, b_ada.reshape(L, 1, N))


def _ffn_kernel(x_ref, mod_ref, g_ref, wa_ref, wb_ref, wd_ref, gf_ref, o_ref, h_ref, *, sub, final_norm):
    j = pl.program_id(2)
    nj = pl.num_programs(2)

    @pl.when(j == 0)
    def _():
        sh = mod_ref[0, 3 * sub:3 * sub + 1, :]
        sc = mod_ref[0, 3 * sub + 1:3 * sub + 2, :]
        h_ref[...] = _adaln(x_ref[0], g_ref[...], sc, sh).astype(BF16)

    h = h_ref[...]
    a = _dot(h, wa_ref[...])
    b = _dot(h, wb_ref[...])
    g = (a * jax.nn.sigmoid(a) * b).astype(BF16)
    contrib = _dot(g, wd_ref[...])

    @pl.when(j == 0)
    def _():
        o_ref[0] = contrib

    @pl.when(j > 0)
    def _():
        o_ref[0] += contrib

    @pl.when(j == nj - 1)
    def _():
        ga = mod_ref[0, 3 * sub + 2:3 * sub + 3, :]
        y = x_ref[0] + 0.5 * ga * o_ref[0]
        if final_norm:
            y = _rms(y, y.shape[-1]) * gf_ref[...]
        o_ref[0] = y


def _ffn_call(x, mod, g_pre, w_up, w_down, g_fin, *, sub, final_norm, tm, tf):
    B, S, D = x.shape
    nf = D_FF // tf
    kern = functools.partial(_ffn_kernel, sub=sub, final_norm=final_norm)
    return pl.pallas_call(
        kern,
        grid=(B, S // tm, nf),
        in_specs=[
            pl.BlockSpec((1, tm, D), lambda b, i, j: (b, i, 0)),
            pl.BlockSpec((1, N_MOD, D), lambda b, i, j: (b, 0, 0)),
            pl.BlockSpec((1, D), lambda b, i, j: (0, 0)),
            pl.BlockSpec((D, tf), lambda b, i, j: (0, j)),
            pl.BlockSpec((D, tf), lambda b, i, j: (0, j + nf)),
            pl.BlockSpec((tf, D), lambda b, i, j: (j, 0)),
            pl.BlockSpec((1, D), lambda b, i, j: (0, 0)),
        ],
        out_specs=pl.BlockSpec((1, tm, D), lambda b, i, j: (b, i, 0)),
        out_shape=jax.ShapeDtypeStruct((B, S, D), F32),
        scratch_shapes=[pltpu.VMEM((tm, D), BF16)],
        compiler_params=_params(("arbitrary", "arbitrary", "arbitrary")),
        name="ffn",
    )(x, mod, g_pre, w_up, w_up, w_down, g_fin)


def _proj_kernel(x_ref, mod_ref, g_ref, win_ref, gcq_ref, gckv_ref, wuq_ref, wukv_ref, gq_ref, gk_ref,
                 gdq_ref, gdk_ref, bd_ref, cos_ref, slo_ref, shi_ref,
                 q_ref, k_ref, v_ref, dq_ref, dk_ref, dv_ref, glu_ref):
    sh = mod_ref[0, 3:4, :]
    sc = mod_ref[0, 4:5, :]
    h = _adaln(x_ref[0], g_ref[...], sc, sh).astype(BF16)
    cos = cos_ref[...]
    slo = slo_ref[...]
    shi = shi_ref[...]

    pa = _dot(h, win_ref[:, SEC_A[0]:SEC_A[1]])
    cq = pa[:, 0:MLA_Q_RANK]
    ckv = pa[:, MLA_Q_RANK:MLA_Q_RANK + MLA_KV_RANK]
    kr = pa[:, MLA_Q_RANK + MLA_KV_RANK:SEC_A[1]]
    cqn = (_rms(cq, MLA_Q_RANK) * gcq_ref[...]).astype(BF16)
    ckvn = (_rms(ckv, MLA_KV_RANK) * gckv_ref[...]).astype(BF16)
    qraw = _dot(cqn, wuq_ref[...])
    kv = _dot(ckvn, wukv_ref[...])
    kr_ss = jnp.sum(kr * kr, axis=-1, keepdims=True)
    gq = gq_ref[...]
    gk = gk_ref[...]
    for hd in range(MLA_HEADS):
        c0 = hd * MLA_QK_PAD
        qh = qraw[:, c0:c0 + MLA_QK_PAD]
        qn = _rms(qh, MLA_QK) * gq
        q_ref[0, hd, :, 0:LANE] = qn[:, 0:LANE].astype(BF16)
        q_ref[0, hd, :, LANE:2 * LANE] = _rope(qn[:, LANE:2 * LANE], cos, slo, shi).astype(BF16)
        kn = kv[:, c0:c0 + LANE]
        inv = lax.rsqrt((jnp.sum(kn * kn, axis=-1, keepdims=True) + kr_ss) * (1.0 / MLA_QK) + EPS)
        k_ref[0, hd, :, 0:LANE] = (kn * inv * gk[:, 0:LANE]).astype(BF16)
        k_ref[0, hd, :, LANE:2 * LANE] = _rope(kr * inv * gk[:, LANE:2 * LANE], cos, slo, shi).astype(BF16)
        v_ref[0, hd] = kv[:, c0 + LANE:c0 + 2 * LANE].astype(BF16)

    bd = bd_ref[...]
    for sec, g_t, o_ref in ((SEC_DQ, gdq_ref, dq_ref), (SEC_DK, gdk_ref, dk_ref)):
        z = _dot(h, win_ref[:, sec[0]:sec[1]])
        zz = z * z
        zz_hi = zz.astype(BF16)
        zz_lo = (zz - zz_hi.astype(F32)).astype(BF16)
        ss = _dot(zz_hi, bd) + _dot(zz_lo, bd)
        zn = z * lax.rsqrt(ss * (1.0 / DIFF_DK) + EPS) * g_t[...]
        for hd in range(DIFF_HEADS):
            o_ref[0, hd] = _rope(zn[:, hd * LANE:(hd + 1) * LANE], cos, slo, shi).astype(BF16)
    z = _dot(h, win_ref[:, SEC_DV[0]:SEC_DV[1]])
    for hd in range(DIFF_HEADS):
        dv_ref[0, hd] = z[:, hd * LANE:(hd + 1) * LANE].astype(BF16)

    cu = _dot(h, win_ref[:, SEC_CU[0]:SEC_CU[1]])
    glu_ref[0] = cu[:, 0:CONV_CH] * jax.nn.sigmoid(cu[:, CONV_CH:2 * CONV_CH])


def _proj_call(x, mod, g_pre, lw, rope_tabs, *, tm):
    B, S, D = x.shape
    H = MLA_HEADS
    const = lambda shape: pl.BlockSpec(shape, lambda b, i: (0,) * len(shape))
    tab = pl.BlockSpec((tm, LANE), lambda b, i: (i, 0))
    head_out = lambda w: pl.BlockSpec((1, H, tm, w), lambda b, i: (b, 0, i, 0))
    head_shape = lambda w: jax.ShapeDtypeStruct((B, H, S, w), BF16)
    return pl.pallas_call(
        _proj_kernel,
        grid=(B, S // tm),
        in_specs=[
            pl.BlockSpec((1, tm, D), lambda b, i: (b, i, 0)),
            pl.BlockSpec((1, N_MOD, D), lambda b, i: (b, 0, 0)),
            const((1, D)),
            const((D, IN_COLS_PAD)),
            const((1, MLA_Q_RANK)),
            const((1, MLA_KV_RANK)),
            const((MLA_Q_RANK, H * MLA_QK_PAD)),
            const((MLA_KV_RANK, H * MLA_QK_PAD)),
            const((1, MLA_QK_PAD)),
            const((1, MLA_QK_PAD)),
            const((1, DIFF_HEADS * LANE)),
            const((1, DIFF_HEADS * LANE)),
            const((DIFF_HEADS * LANE, DIFF_HEADS * LANE)),
            tab, tab, tab,
        ],
        out_specs=[head_out(MLA_QK_PAD), head_out(MLA_QK_PAD), head_out(LANE),
                   head_out(LANE), head_out(LANE), head_out(LANE),
                   pl.BlockSpec((1, tm, CONV_CH), lambda b, i: (b, i, 0))],
        out_shape=[head_shape(MLA_QK_PAD), head_shape(MLA_QK_PAD), head_shape(LANE),
                   head_shape(LANE), head_shape(LANE), head_shape(LANE),
                   jax.ShapeDtypeStruct((B, S, CONV_CH), F32)],
        compiler_params=_params(("arbitrary", "arbitrary")),
        name="proj",
    )(x, mod, g_pre, lw["w_in"], lw["g_cq"], lw["g_ckv"], lw["w_uq"], lw["w_ukv"], lw["g_q"], lw["g_k"],
      lw["g_dq"], lw["g_dk"], lw["bd"], *rope_tabs)


def _softmax_step(s, vs, m_ref, l_ref, acc_ref):
    m_prev = m_ref[...]
    m_new = jnp.maximum(m_prev, jnp.max(s, axis=-1, keepdims=True))
    alpha = jnp.exp(m_prev - m_new)
    p = jnp.exp(s - m_new)
    l_ref[...] = alpha * l_ref[...] + jnp.sum(p, axis=-1, keepdims=True)
    acc_ref[...] = alpha * acc_ref[...] + _dot(p.astype(BF16), vs)
    m_ref[...] = m_new


def _mla_attn_kernel(q_ref, k_ref, v_ref, o_ref, m_ref, l_ref, acc_ref, *, tk):
    S = k_ref.shape[2]
    q = q_ref[0, 0]
    m_ref[...] = jnp.full(m_ref.shape, -jnp.inf, F32)
    l_ref[...] = jnp.zeros(l_ref.shape, F32)
    acc_ref[...] = jnp.zeros(acc_ref.shape, F32)

    def body(c, carry):
        r0 = pl.multiple_of(c * tk, tk)
        s = _dot_nt(q, k_ref[0, 0, pl.ds(r0, tk), :])
        _softmax_step(s, v_ref[0, 0, pl.ds(r0, tk), :], m_ref, l_ref, acc_ref)
        return carry

    lax.fori_loop(0, S // tk, body, 0)
    o_ref[0] = (acc_ref[...] / l_ref[...]).astype(o_ref.dtype)


def _diff_attn_kernel(q_ref, k_ref, v_ref, lam_ref, gsub_ref, o_ref,
                      m1_ref, l1_ref, a1_ref, m2_ref, l2_ref, a2_ref, *, tk, lam_init):
    S = k_ref.shape[2]
    q = q_ref[0, 0]
    lane = lax.broadcasted_iota(jnp.int32, q.shape, 1)
    zero = jnp.zeros_like(q)
    q1 = jnp.where(lane < DIFF_DK, q, zero)
    q2 = jnp.where(lane >= DIFF_DK, q, zero)
    for m_ref, l_ref, a_ref in ((m1_ref, l1_ref, a1_ref), (m2_ref, l2_ref, a2_ref)):
        m_ref[...] = jnp.full(m_ref.shape, -jnp.inf, F32)
        l_ref[...] = jnp.zeros(l_ref.shape, F32)
        a_ref[...] = jnp.zeros(a_ref.shape, F32)

    def body(c, carry):
        r0 = pl.multiple_of(c * tk, tk)
        ks = k_ref[0, 0, pl.ds(r0, tk), :]
        vs = v_ref[0, 0, pl.ds(r0, tk), :]
        _softmax_step(_dot_nt(q1, ks), vs, m1_ref, l1_ref, a1_ref)
        _softmax_step(_dot_nt(q2, ks), vs, m2_ref, l2_ref, a2_ref)
        return carry

    lax.fori_loop(0, S // tk, body, 0)
    lv = lam_ref[...]
    lam = (jnp.exp(jnp.sum(lv[0:1] * lv[1:2], axis=-1, keepdims=True))
           - jnp.exp(jnp.sum(lv[2:3] * lv[3:4], axis=-1, keepdims=True)) + lam_init)
    o = a1_ref[...] / l1_ref[...] - lam * (a2_ref[...] / l2_ref[...])
    o = _rms(o, DIFF_DV) * gsub_ref[...] * (1.0 - lam_init)
    o_ref[0] = o.astype(o_ref.dtype)


def _shifted_step(s, bound, vs, ls_ref, acc_ref):
    p = jnp.exp(s - bound)
    part = p[:, 0:LANE]
    for t in range(1, p.shape[1] // LANE):
        part = part + p[:, t * LANE:(t + 1) * LANE]
    ls_ref[...] += part
    acc_ref[...] += _dot(p.astype(BF16), vs)


def _mla_shifted_kernel(b_ref, q_ref, k_ref, v_ref, o_ref, ls_ref, acc_ref, *, tk):
    S = k_ref.shape[2]
    q = q_ref[0, 0]
    bound = b_ref[0]
    ls_ref[...] = jnp.zeros(ls_ref.shape, F32)
    acc_ref[...] = jnp.zeros(acc_ref.shape, F32)

    def body(c, carry):
        r0 = pl.multiple_of(c * tk, tk)
        s = _dot_nt(q, k_ref[0, 0, pl.ds(r0, tk), :])
        _shifted_step(s, bound, v_ref[0, 0, pl.ds(r0, tk), :], ls_ref, acc_ref)
        return carry

    lax.fori_loop(0, S // tk, body, 0)
    l = jnp.sum(ls_ref[...], axis=-1, keepdims=True)
    o_ref[0] = (acc_ref[...] / l).astype(o_ref.dtype)


def _diff_shifted_kernel(b_ref, q_ref, k_ref, v_ref, lam_ref, gsub_ref, o_ref,
                         ls1_ref, a1_ref, ls2_ref, a2_ref, *, tk, lam_init):
    S = k_ref.shape[2]
    q = q_ref[0, 0]
    bound = b_ref[0]
    lane = lax.broadcasted_iota(jnp.int32, q.shape, 1)
    zero = jnp.zeros_like(q)
    q1 = jnp.where(lane < DIFF_DK, q, zero)
    q2 = jnp.where(lane >= DIFF_DK, q, zero)
    for r in (ls1_ref, a1_ref, ls2_ref, a2_ref):
        r[...] = jnp.zeros(r.shape, F32)

    def body(c, carry):
        r0 = pl.multiple_of(c * tk, tk)
        ks = k_ref[0, 0, pl.ds(r0, tk), :]
        vs = v_ref[0, 0, pl.ds(r0, tk), :]
        _shifted_step(_dot_nt(q1, ks), bound, vs, ls1_ref, a1_ref)
        _shifted_step(_dot_nt(q2, ks), bound, vs, ls2_ref, a2_ref)
        return carry

    lax.fori_loop(0, S // tk, body, 0)
    lv = lam_ref[...]
    lam = (jnp.exp(jnp.sum(lv[0:1] * lv[1:2], axis=-1, keepdims=True))
           - jnp.exp(jnp.sum(lv[2:3] * lv[3:4], axis=-1, keepdims=True)) + lam_init)
    l1 = jnp.sum(ls1_ref[...], axis=-1, keepdims=True)
    l2 = jnp.sum(ls2_ref[...], axis=-1, keepdims=True)
    o = a1_ref[...] / l1 - lam * (a2_ref[...] / l2)
    o = _rms(o, DIFF_DV) * gsub_ref[...] * (1.0 - lam_init)
    o_ref[0] = o.astype(o_ref.dtype)


def _attn_call(q, k, v, extra, bound, *, tq, tk, lam_init=None):
    B, H, S, dqk = q.shape
    dv = v.shape[-1]
    tk_safe = min(tk, 512)
    stat = pltpu.VMEM((tq, 1), F32)
    acc = pltpu.VMEM((tq, dv), F32)
    qkv_specs = [
        pl.BlockSpec((1, 1, tq, dqk), lambda b, h, i: (b, h, i, 0)),
        pl.BlockSpec((1, 1, S, dqk), lambda b, h, i: (b, h, 0, 0)),
        pl.BlockSpec((1, 1, S, dv), lambda b, h, i: (b, h, 0, 0)),
    ]
    extra_specs = [pl.BlockSpec(e.shape, lambda b, h, i: (0, 0)) for e in extra]
    smem = pl.BlockSpec(memory_space=pltpu.SMEM)
    if lam_init is None:
        safe_kern = functools.partial(_mla_attn_kernel, tk=tk_safe)
        fast_kern = functools.partial(_mla_shifted_kernel, tk=tk)
        safe_scratch, fast_scratch = [stat, stat, acc], [acc, acc]
        name = "mla_attn"
    else:
        safe_kern = functools.partial(_diff_attn_kernel, tk=tk_safe, lam_init=lam_init)
        fast_kern = functools.partial(_diff_shifted_kernel, tk=tk, lam_init=lam_init)
        safe_scratch, fast_scratch = [stat, stat, acc, stat, stat, acc], [acc, acc, acc, acc]
        name = "diff_attn"
    common = dict(
        grid=(B, H, S // tq),
        out_specs=pl.BlockSpec((1, tq, dv), lambda b, h, i: (b, i, h)),
        out_shape=jax.ShapeDtypeStruct((B, S, H * dv), BF16),
        compiler_params=_params(("arbitrary", "arbitrary", "arbitrary")),
    )

    def fast(bound, q, k, v, *extra):
        return pl.pallas_call(fast_kern, in_specs=[smem] + qkv_specs + extra_specs, scratch_shapes=fast_scratch,
                              name=name + "_shifted", **common)(bound.reshape(1), q, k, v, *extra)

    def safe(bound, q, k, v, *extra):
        return pl.pallas_call(safe_kern, in_specs=qkv_specs + extra_specs, scratch_shapes=safe_scratch,
                              name=name, **common)(q, k, v, *extra)

    return lax.cond(bound <= MAX_SHIFT_BOUND, fast, safe, bound, q, k, v, *extra)


def _conv_kernel(prev_ref, cur_ref, next_ref, w_ref, b_ref, g_ref, beta_ref, o_ref, ext_ref, *, rows):
    i = pl.program_id(1)
    ni = pl.num_programs(1)
    tm = cur_ref.shape[1]
    halo = CONV_HALO
    prev = prev_ref[0]
    nxt = next_ref[0]
    ext_ref[0:halo, :] = jnp.where(i > 0, prev, jnp.zeros_like(prev))
    ext_ref[halo:halo + tm, :] = cur_ref[0]
    ext_ref[halo + tm:2 * halo + tm, :] = jnp.where(i < ni - 1, nxt, jnp.zeros_like(nxt))
    w = w_ref[...]
    bias = b_ref[...]
    g = g_ref[...]
    beta = beta_ref[...]
    off = halo - CONV_WIDTH // 2

    for r0 in range(0, tm, rows):
        acc = jnp.zeros((rows, CONV_CH), F32)
        for j in range(CONV_WIDTH):
            acc = acc + ext_ref[r0 + j + off:r0 + j + off + rows, :] * w[j:j + 1, :]
        acc = acc + bias
        mu = jnp.mean(acc, axis=-1, keepdims=True)
        d = acc - mu
        var = jnp.mean(d * d, axis=-1, keepdims=True)
        y = d * lax.rsqrt(var + EPS) * g + beta
        o_ref[0, r0:r0 + rows, :] = (y * jax.nn.sigmoid(y)).astype(o_ref.dtype)


def _conv_call(glu, w, b, g, beta, *, tm, rows=64):
    B, S, C = glu.shape
    hb = tm // CONV_HALO
    nh = S // CONV_HALO
    const = lambda shape: pl.BlockSpec(shape, lambda b_, i: (0, 0))
    return pl.pallas_call(
        functools.partial(_conv_kernel, rows=min(rows, tm)),
        grid=(B, S // tm),
        in_specs=[
            pl.BlockSpec((1, CONV_HALO, C), lambda b_, i: (b_, jnp.maximum(i * hb - 1, 0), 0)),
            pl.BlockSpec((1, tm, C), lambda b_, i: (b_, i, 0)),
            pl.BlockSpec((1, CONV_HALO, C), lambda b_, i: (b_, jnp.minimum((i + 1) * hb, nh - 1), 0)),
            const(w.shape), const((1, C)), const((1, C)), const((1, C)),
        ],
        out_specs=pl.BlockSpec((1, tm, C), lambda b_, i: (b_, i, 0)),
        out_shape=jax.ShapeDtypeStruct((B, S, C), BF16),
        scratch_shapes=[pltpu.VMEM((tm + 2 * CONV_HALO, C), F32)],
        compiler_params=_params(("arbitrary", "arbitrary")),
        name="conv",
    )(glu, glu, glu, w, b, g, beta)


def _out_kernel(x_ref, mod_ref, mla_ref, dif_ref, cnv_ref, w_ref, o_ref):
    n1 = mla_ref.shape[-1]
    n2 = n1 + dif_ref.shape[-1]
    acc = (_dot(mla_ref[0], w_ref[0:n1, :]) + _dot(dif_ref[0], w_ref[n1:n2, :])
           + _dot(cnv_ref[0], w_ref[n2:, :]))
    o_ref[0] = x_ref[0] + mod_ref[0, 5:6, :] * acc


def _out_call(x, mod, o_mla, o_dif, o_cnv, w_out, *, tm):
    B, S, D = x.shape
    tok = lambda w: pl.BlockSpec((1, tm, w), lambda b, i: (b, i, 0))
    return pl.pallas_call(
        _out_kernel,
        grid=(B, S // tm),
        in_specs=[tok(D), pl.BlockSpec((1, N_MOD, D), lambda b, i: (b, 0, 0)),
                  tok(o_mla.shape[-1]), tok(o_dif.shape[-1]), tok(o_cnv.shape[-1]),
                  pl.BlockSpec(w_out.shape, lambda b, i: (0, 0))],
        out_specs=tok(D),
        out_shape=jax.ShapeDtypeStruct((B, S, D), F32),
        compiler_params=_params(("arbitrary", "arbitrary")),
        name="out",
    )(x, mod, o_mla, o_dif, o_cnv, w_out)


def _rope_tables(S):
    half = DIFF_DK // 2
    inv_freq = jnp.float32(ROPE_THETA) ** (-jnp.arange(half, dtype=F32) * (2.0 / DIFF_DK))
    ang = jnp.arange(S, dtype=jnp.int32).astype(F32)[:, None] * inv_freq[None, :]
    c, s, z = jnp.cos(ang), jnp.sin(ang), jnp.zeros_like(ang)
    return (jnp.concatenate([c, c, c, c], axis=1),
            jnp.concatenate([-s, z, -s, z], axis=1),
            jnp.concatenate([z, s, z, s], axis=1))


def _layer_weights(l, w_in, g_cq, g_ckv, w_uq, w_ukv, g_mla_q, g_mla_k, g_diff_q, g_diff_k,
                   lam_q1, lam_k1, lam_q2, lam_k2, g_diff_sub, w_dw, b_dw, g_conv, b_conv, w_out):
    wi = w_in[l]
    o_rope = MLA_Q_RANK + MLA_KV_RANK
    o_dq = o_rope + MLA_ROPE
    w_in_p = jnp.concatenate(
        [wi[:, :o_dq], jnp.zeros((D_MODEL, LANE - MLA_ROPE), wi.dtype), wi[:, o_dq:]], axis=1).astype(BF16)
    w_uq_p = jnp.pad(w_uq[l].reshape(MLA_Q_RANK, MLA_HEADS, MLA_QK),
                     ((0, 0), (0, 0), (0, MLA_QK_PAD - MLA_QK))).reshape(MLA_Q_RANK, -1).astype(BF16)
    pad_qk = lambda g: jnp.pad(g, (0, MLA_QK_PAD - MLA_QK)).reshape(1, MLA_QK_PAD)
    return dict(
        w_in=w_in_p, g_cq=g_cq[l].reshape(1, -1), g_ckv=g_ckv[l].reshape(1, -1),
        w_uq=w_uq_p, w_ukv=w_ukv[l].astype(BF16),
        g_q=pad_qk(g_mla_q[l] * (1.0 / math.sqrt(MLA_QK))), g_k=pad_qk(g_mla_k[l]),
        g_dq=jnp.tile(g_diff_q[l] * (1.0 / math.sqrt(DIFF_DK)), 2 * DIFF_HEADS).reshape(1, -1),
        g_dk=jnp.tile(g_diff_k[l], 2 * DIFF_HEADS).reshape(1, -1),
        bd=jnp.asarray(np.kron(np.eye(2 * DIFF_HEADS), np.ones((DIFF_DK, DIFF_DK))), BF16),
        lam=jnp.stack([lam_q1[l], lam_k1[l], lam_q2[l], lam_k2[l]]),
        b_mla=1.01 * math.sqrt(MLA_QK) * jnp.max(jnp.abs(g_mla_q[l])) * jnp.max(jnp.abs(g_mla_k[l])),
        b_dif=1.01 * math.sqrt(DIFF_DK) * jnp.max(jnp.abs(g_diff_q[l])) * jnp.max(jnp.abs(g_diff_k[l])),
        g_sub=g_diff_sub[l].reshape(1, -1),
        w_dw=jnp.pad(w_dw[l], ((0, 1), (0, 0))), b_dw=b_dw[l].reshape(1, -1),
        g_conv=g_conv[l].reshape(1, -1), b_conv=b_conv[l].reshape(1, -1),
        w_out=w_out[l].astype(BF16),
    )


def _tile(S, t):
    return min(S, t)


def kernel(x_prompt, x_sample, c_prompt, c_sample, w_ada, b_ada, g_norm, w_ffn1_in, w_ffn1_out, w_ffn2_in, w_ffn2_out, w_in, g_cq, g_ckv, w_uq, w_ukv, g_mla_q, g_mla_k, g_diff_q, g_diff_k, lam_q1, lam_k1, lam_q2, lam_k2, g_diff_sub, w_dw, b_dw, g_conv, b_conv, w_out):
    L = w_ada.shape[0]
    bp, bs = c_prompt.shape[0], c_sample.shape[0]
    rows = -(-(bp + bs) // 8) * 8
    c_all = jnp.pad(jnp.concatenate([c_prompt, c_sample], axis=0), ((0, rows - bp - bs), (0, 0)))
    mod_all = _ada_call(c_all, w_ada, b_ada)

    up1, dn1 = w_ffn1_in.astype(BF16), w_ffn1_out.astype(BF16)
    up2, dn2 = w_ffn2_in.astype(BF16), w_ffn2_out.astype(BF16)
    layers = [_layer_weights(l, w_in, g_cq, g_ckv, w_uq, w_ukv, g_mla_q, g_mla_k, g_diff_q, g_diff_k,
                             lam_q1, lam_k1, lam_q2, lam_k2, g_diff_sub, w_dw, b_dw, g_conv, b_conv, w_out)
              for l in range(L)]

    def trunk(x, row0):
        B, S, D = x.shape
        tm = _tile(S, 512)
        tabs = _rope_tables(S)
        for l in range(L):
            lw = layers[l]
            mod = mod_all[l, row0:row0 + B].reshape(B, N_MOD, D)
            gn = g_norm[l]
            x = _ffn_call(x, mod, gn[0:1], up1[l], dn1[l], gn[3:4], sub=0, final_norm=False, tm=tm, tf=512)
            q, k, v, dq, dk, dv, glu = _proj_call(x, mod, gn[1:2], lw, tabs, tm=tm)
            o_mla = _attn_call(q, k, v, (), lw["b_mla"], tq=tm, tk=_tile(S, 2048))
            lam_init = 0.8 - 0.6 * math.exp(-0.3 * l)
            o_dif = _attn_call(dq, dk, dv, (lw["lam"], lw["g_sub"]), lw["b_dif"], tq=tm, tk=_tile(S, 2048),
                               lam_init=lam_init)
            o_cnv = _conv_call(glu, lw["w_dw"], lw["b_dw"], lw["g_conv"], lw["b_conv"], tm=tm)
            x = _out_call(x, mod, o_mla, o_dif, o_cnv, lw["w_out"], tm=tm)
            x = _ffn_call(x, mod, gn[2:3], up2[l], dn2[l], gn[3:4], sub=2, final_norm=True, tm=tm, tf=512)
        return x

    return (trunk(x_prompt, 0), trunk(x_sample, bp))
```

```python
import functools
import math

import numpy as np
import jax
import jax.numpy as jnp
from jax import lax
from jax.experimental import pallas as pl
from jax.experimental.pallas import tpu as pltpu

F32 = jnp.float32
BF16 = jnp.bfloat16

D_MODEL = 2048
MLA_HEADS = 6
MLA_Q_RANK = 512
MLA_KV_RANK = 256
MLA_NOPE = 128
MLA_ROPE = 64
MLA_V = 128
MLA_QK = MLA_NOPE + MLA_ROPE
MLA_QK_PAD = 256
DIFF_HEADS = 6
DIFF_DK = 64
DIFF_DV = 128
CONV_CH = 512
CONV_WIDTH = 31
CONV_HALO = 16
D_FF = 5632
N_MOD = 9
ROPE_THETA = 10000.0
EPS = 1e-6
LANE = 128

SEC_A = (0, 896)
SEC_DQ = (896, 1664)
SEC_DK = (1664, 2432)
SEC_DV = (2432, 3200)
SEC_CU = (3200, 4224)
IN_COLS_PAD = 4224

VMEM_LIMIT = 56 * 1024 * 1024
MAX_SHIFT_BOUND = 40.0


def _dot(a, b):
    return jnp.dot(a, b, preferred_element_type=F32)


def _dot_nt(a, b):
    return lax.dot_general(a, b, (((1,), (1,)), ((), ())), preferred_element_type=F32)


def _rms(x, width):
    return x * lax.rsqrt(jnp.sum(x * x, axis=-1, keepdims=True) * (1.0 / width) + EPS)


def _adaln(x, g, sc, sh):
    return (_rms(x, x.shape[-1]) * g) * (1.0 + sc) + sh


def _rope(v, cos, sin_lo, sin_hi):
    return v * cos + pltpu.roll(v, 96, 1) * sin_lo + pltpu.roll(v, 32, 1) * sin_hi


def _params(sem, vmem=VMEM_LIMIT):
    return pltpu.CompilerParams(dimension_semantics=sem, vmem_limit_bytes=vmem)


def _ada_kernel(c_ref, w_ref, b_ref, o_ref):
    c = c_ref[...]
    s = c * jax.nn.sigmoid(c)
    s_hi = s.astype(BF16)
    s_lo = (s - s_hi.astype(F32)).astype(BF16)
    w = w_ref[0]
    w_hi = w.astype(BF16)
    w_lo = (w - w_hi.astype(F32)).astype(BF16)
    o_ref[0] = _dot(s_hi, w_hi) + _dot(s_lo, w_hi) + _dot(s_hi, w_lo) + b_ref[0]


def _ada_call(c_all, w_ada, b_ada, tn=512):
    L, D, N = w_ada.shape
    R = c_all.shape[0]
    return pl.pallas_call(
        _ada_kernel,
        grid=(L, N // tn),
        in_specs=[
            pl.BlockSpec((R, D), lambda l, n: (0, 0)),
            pl.BlockSpec((1, D, tn), lambda l, n: (l, 0, n)),
            pl.BlockSpec((1, 1, tn), lambda l, n: (l, 0, n)),
        ],
        out_specs=pl.BlockSpec((1, R, tn), lambda l, n: (l, 0, n)),
        out_shape=jax.ShapeDtypeStruct((L, R, N), F32),
        compiler_params=_params(("arbitrary", "arbitrary")),
        name="ada",
    )(c_all, w_ada, b_ada.reshape(L, 1, N))


def _ffn_kernel(x_ref, mod_ref, g_ref, wa_ref, wb_ref, wd_ref, gf_ref, o_ref, h_ref, act_ref, *, sub, final_norm):
    j = pl.program_id(2)
    nf = pl.num_programs(2) - 1

    def up(slot):
        h = h_ref[...]
        a = _dot(h, wa_ref[...])
        b = _dot(h, wb_ref[...])
        act_ref[slot] = (a * jax.nn.sigmoid(a) * b).astype(BF16)

    def down(slot):
        return _dot(act_ref[slot], wd_ref[...])

    @pl.when(j == 0)
    def _():
        sh = mod_ref[0, 3 * sub:3 * sub + 1, :]
        sc = mod_ref[0, 3 * sub + 1:3 * sub + 2, :]
        h_ref[...] = _adaln(x_ref[0], g_ref[...], sc, sh).astype(BF16)
        o_ref[0] = jnp.zeros(o_ref.shape[1:], F32)
        up(0)

    @pl.when((j > 0) & (j < nf))
    def _():
        up(j % 2)
        o_ref[0] += down((j - 1) % 2)

    @pl.when(j == nf)
    def _():
        ga = mod_ref[0, 3 * sub + 2:3 * sub + 3, :]
        y = x_ref[0] + 0.5 * ga * (o_ref[0] + down((nf - 1) % 2))
        if final_norm:
            y = _rms(y, y.shape[-1]) * gf_ref[...]
        o_ref[0] = y


def _ffn_call(x, mod, g_pre, w_up, w_down, g_fin, *, sub, final_norm, tm, tf):
    B, S, D = x.shape
    nf = D_FF // tf
    kern = functools.partial(_ffn_kernel, sub=sub, final_norm=final_norm)
    return pl.pallas_call(
        kern,
        grid=(B, S // tm, nf + 1),
        in_specs=[
            pl.BlockSpec((1, tm, D), lambda b, i, j: (b, i, 0)),
            pl.BlockSpec((1, N_MOD, D), lambda b, i, j: (b, 0, 0)),
            pl.BlockSpec((1, D), lambda b, i, j: (0, 0)),
            pl.BlockSpec((D, tf), lambda b, i, j: (0, jnp.minimum(j, nf - 1))),
            pl.BlockSpec((D, tf), lambda b, i, j: (0, jnp.minimum(j, nf - 1) + nf)),
            pl.BlockSpec((tf, D), lambda b, i, j: (jnp.maximum(j - 1, 0), 0)),
            pl.BlockSpec((1, D), lambda b, i, j: (0, 0)),
        ],
        out_specs=pl.BlockSpec((1, tm, D), lambda b, i, j: (b, i, 0)),
        out_shape=jax.ShapeDtypeStruct((B, S, D), F32),
        scratch_shapes=[pltpu.VMEM((tm, D), BF16), pltpu.VMEM((2, tm, tf), BF16)],
        compiler_params=_params(("arbitrary", "arbitrary", "arbitrary")),
        name="ffn",
    )(x, mod, g_pre, w_up, w_up, w_down, g_fin)


def _proj_kernel(x_ref, mod_ref, g_ref, win_ref, gcq_ref, gckv_ref, wuq_ref, wukv_ref, gq_ref, gk_ref,
                 gdq_ref, gdk_ref, bd_ref, cos_ref, slo_ref, shi_ref,
                 q_ref, k_ref, v_ref, dq_ref, dk_ref, dv_ref, glu_ref):
    sh = mod_ref[0, 3:4, :]
    sc = mod_ref[0, 4:5, :]
    h = _adaln(x_ref[0], g_ref[...], sc, sh).astype(BF16)
    cos = cos_ref[...]
    slo = slo_ref[...]
    shi = shi_ref[...]

    pa = _dot(h, win_ref[:, SEC_A[0]:SEC_A[1]])
    cq = pa[:, 0:MLA_Q_RANK]
    ckv = pa[:, MLA_Q_RANK:MLA_Q_RANK + MLA_KV_RANK]
    kr = pa[:, MLA_Q_RANK + MLA_KV_RANK:SEC_A[1]]
    cqn = (_rms(cq, MLA_Q_RANK) * gcq_ref[...]).astype(BF16)
    ckvn = (_rms(ckv, MLA_KV_RANK) * gckv_ref[...]).astype(BF16)
    qraw = _dot(cqn, wuq_ref[...])
    kv = _dot(ckvn, wukv_ref[...])
    kr_ss = jnp.sum(kr * kr, axis=-1, keepdims=True)
    gq = gq_ref[...]
    gk = gk_ref[...]
    for hd in range(MLA_HEADS):
        c0 = hd * MLA_QK_PAD
        qh = qraw[:, c0:c0 + MLA_QK_PAD]
        qn = _rms(qh, MLA_QK) * gq
        q_ref[0, hd, :, 0:LANE] = qn[:, 0:LANE].astype(BF16)
        q_ref[0, hd, :, LANE:2 * LANE] = _rope(qn[:, LANE:2 * LANE], cos, slo, shi).astype(BF16)
        kn = kv[:, c0:c0 + LANE]
        inv = lax.rsqrt((jnp.sum(kn * kn, axis=-1, keepdims=True) + kr_ss) * (1.0 / MLA_QK) + EPS)
        k_ref[0, hd, :, 0:LANE] = (kn * inv * gk[:, 0:LANE]).astype(BF16)
        k_ref[0, hd, :, LANE:2 * LANE] = _rope(kr * inv * gk[:, LANE:2 * LANE], cos, slo, shi).astype(BF16)
        v_ref[0, hd] = kv[:, c0 + LANE:c0 + 2 * LANE].T.astype(BF16)

    bd = bd_ref[...]
    for sec, g_t, o_ref in ((SEC_DQ, gdq_ref, dq_ref), (SEC_DK, gdk_ref, dk_ref)):
        z = _dot(h, win_ref[:, sec[0]:sec[1]])
        zz = z * z
        zz_hi = zz.astype(BF16)
        zz_lo = (zz - zz_hi.astype(F32)).astype(BF16)
        ss = _dot(zz_hi, bd) + _dot(zz_lo, bd)
        zn = z * lax.rsqrt(ss * (1.0 / DIFF_DK) + EPS) * g_t[...]
        for hd in range(DIFF_HEADS):
            o_ref[0, hd] = _rope(zn[:, hd * LANE:(hd + 1) * LANE], cos, slo, shi).astype(BF16)
    z = _dot(h, win_ref[:, SEC_DV[0]:SEC_DV[1]])
    for hd in range(DIFF_HEADS):
        dv_ref[0, hd] = z[:, hd * LANE:(hd + 1) * LANE].T.astype(BF16)

    cu = _dot(h, win_ref[:, SEC_CU[0]:SEC_CU[1]])
    glu_ref[0] = cu[:, 0:CONV_CH] * jax.nn.sigmoid(cu[:, CONV_CH:2 * CONV_CH])


def _proj_call(x, mod, g_pre, lw, rope_tabs, *, tm):
    B, S, D = x.shape
    H = MLA_HEADS
    const = lambda shape: pl.BlockSpec(shape, lambda b, i: (0,) * len(shape))
    tab = pl.BlockSpec((tm, LANE), lambda b, i: (i, 0))
    head_out = lambda w: pl.BlockSpec((1, H, tm, w), lambda b, i: (b, 0, i, 0))
    head_shape = lambda w: jax.ShapeDtypeStruct((B, H, S, w), BF16)
    val_out = pl.BlockSpec((1, H, LANE, tm), lambda b, i: (b, 0, 0, i))
    val_shape = jax.ShapeDtypeStruct((B, H, LANE, S), BF16)
    return pl.pallas_call(
        _proj_kernel,
        grid=(B, S // tm),
        in_specs=[
            pl.BlockSpec((1, tm, D), lambda b, i: (b, i, 0)),
            pl.BlockSpec((1, N_MOD, D), lambda b, i: (b, 0, 0)),
            const((1, D)),
            const((D, IN_COLS_PAD)),
            const((1, MLA_Q_RANK)),
            const((1, MLA_KV_RANK)),
            const((MLA_Q_RANK, H * MLA_QK_PAD)),
            const((MLA_KV_RANK, H * MLA_QK_PAD)),
            const((1, MLA_QK_PAD)),
            const((1, MLA_QK_PAD)),
            const((1, DIFF_HEADS * LANE)),
            const((1, DIFF_HEADS * LANE)),
            const((DIFF_HEADS * LANE, DIFF_HEADS * LANE)),
            tab, tab, tab,
        ],
        out_specs=[head_out(MLA_QK_PAD), head_out(MLA_QK_PAD), val_out,
                   head_out(LANE), head_out(LANE), val_out,
                   pl.BlockSpec((1, tm, CONV_CH), lambda b, i: (b, i, 0))],
        out_shape=[head_shape(MLA_QK_PAD), head_shape(MLA_QK_PAD), val_shape,
                   head_shape(LANE), head_shape(LANE), val_shape,
                   jax.ShapeDtypeStruct((B, S, CONV_CH), F32)],
        compiler_params=_params(("arbitrary", "arbitrary")),
        name="proj",
    )(x, mod, g_pre, lw["w_in"], lw["g_cq"], lw["g_ckv"], lw["w_uq"], lw["w_ukv"], lw["g_q"], lw["g_k"],
      lw["g_dq"], lw["g_dk"], lw["bd"], *rope_tabs)


def _softmax_step(s, vt, m_ref, l_ref, acc_ref):
    m_prev = m_ref[...]
    m_new = jnp.maximum(m_prev, jnp.max(s, axis=-1, keepdims=True))
    alpha = jnp.exp(m_prev - m_new)
    p = jnp.exp(s - m_new)
    l_ref[...] = alpha * l_ref[...] + jnp.sum(p, axis=-1, keepdims=True)
    acc_ref[...] = alpha * acc_ref[...] + _dot_nt(p.astype(BF16), vt)
    m_ref[...] = m_new


def _split_maps(q):
    lane = lax.broadcasted_iota(jnp.int32, q.shape, 1)
    zero = jnp.zeros_like(q)
    return jnp.where(lane < DIFF_DK, q, zero), jnp.where(lane >= DIFF_DK, q, zero)


def _lambda(lam_ref, lam_init):
    lv = lam_ref[...]
    return (jnp.exp(jnp.sum(lv[0:1] * lv[1:2], axis=-1, keepdims=True))
            - jnp.exp(jnp.sum(lv[2:3] * lv[3:4], axis=-1, keepdims=True)) + lam_init)


def _mla_attn_kernel(q_ref, k_ref, v_ref, o_ref, m_ref, l_ref, acc_ref, *, tk):
    S = k_ref.shape[2]
    q = q_ref[0, 0]
    m_ref[...] = jnp.full(m_ref.shape, -jnp.inf, F32)
    l_ref[...] = jnp.zeros(l_ref.shape, F32)
    acc_ref[...] = jnp.zeros(acc_ref.shape, F32)

    def body(c, carry):
        r0 = pl.multiple_of(c * tk, tk)
        s = _dot_nt(q, k_ref[0, 0, pl.ds(r0, tk), :])
        _softmax_step(s, v_ref[0, 0, :, pl.ds(r0, tk)], m_ref, l_ref, acc_ref)
        return carry

    lax.fori_loop(0, S // tk, body, 0)
    o_ref[0] = (acc_ref[...] / l_ref[...]).astype(o_ref.dtype)


def _diff_attn_kernel(q_ref, k_ref, v_ref, lam_ref, gsub_ref, o_ref,
                      m1_ref, l1_ref, a1_ref, m2_ref, l2_ref, a2_ref, *, tk, lam_init):
    S = k_ref.shape[2]
    q1, q2 = _split_maps(q_ref[0, 0])
    for m_ref, l_ref, a_ref in ((m1_ref, l1_ref, a1_ref), (m2_ref, l2_ref, a2_ref)):
        m_ref[...] = jnp.full(m_ref.shape, -jnp.inf, F32)
        l_ref[...] = jnp.zeros(l_ref.shape, F32)
        a_ref[...] = jnp.zeros(a_ref.shape, F32)

    def body(c, carry):
        r0 = pl.multiple_of(c * tk, tk)
        ks = k_ref[0, 0, pl.ds(r0, tk), :]
        vt = v_ref[0, 0, :, pl.ds(r0, tk)]
        _softmax_step(_dot_nt(q1, ks), vt, m1_ref, l1_ref, a1_ref)
        _softmax_step(_dot_nt(q2, ks), vt, m2_ref, l2_ref, a2_ref)
        return carry

    lax.fori_loop(0, S // tk, body, 0)
    o = a1_ref[...] / l1_ref[...] - _lambda(lam_ref, lam_init) * (a2_ref[...] / l2_ref[...])
    o = _rms(o, DIFF_DV) * gsub_ref[...] * (1.0 - lam_init)
    o_ref[0] = o.astype(o_ref.dtype)


def _shifted_step(q, ks, vt, bound, ls_ref, acc_ref):
    pt = jnp.exp(_dot_nt(ks, q) - bound)
    part = pt[0:8, :]
    for t in range(1, pt.shape[0] // 8):
        part = part + pt[8 * t:8 * t + 8, :]
    ls_ref[...] += part
    acc_ref[...] += _dot(vt, pt.astype(BF16))


def _mla_shifted_kernel(b_ref, q_ref, k_ref, v_ref, o_ref, ls_ref, acc_ref, *, tk):
    S = k_ref.shape[2]
    q = q_ref[0, 0]
    bound = b_ref[0]
    ls_ref[...] = jnp.zeros(ls_ref.shape, F32)
    acc_ref[...] = jnp.zeros(acc_ref.shape, F32)

    def body(c, carry):
        r0 = pl.multiple_of(c * tk, tk)
        _shifted_step(q, k_ref[0, 0, pl.ds(r0, tk), :], v_ref[0, 0, :, pl.ds(r0, tk)], bound, ls_ref, acc_ref)
        return carry

    lax.fori_loop(0, S // tk, body, 0)
    ot = acc_ref[...] / jnp.sum(ls_ref[...], axis=0, keepdims=True)
    o_ref[0] = ot.T.astype(o_ref.dtype)


def _diff_shifted_kernel(b_ref, q_ref, k_ref, v_ref, lam_ref, gsub_ref, o_ref,
                         ls1_ref, a1_ref, ls2_ref, a2_ref, *, tk, lam_init):
    S = k_ref.shape[2]
    q1, q2 = _split_maps(q_ref[0, 0])
    bound = b_ref[0]
    for r in (ls1_ref, a1_ref, ls2_ref, a2_ref):
        r[...] = jnp.zeros(r.shape, F32)

    def body(c, carry):
        r0 = pl.multiple_of(c * tk, tk)
        ks = k_ref[0, 0, pl.ds(r0, tk), :]
        vt = v_ref[0, 0, :, pl.ds(r0, tk)]
        _shifted_step(q1, ks, vt, bound, ls1_ref, a1_ref)
        _shifted_step(q2, ks, vt, bound, ls2_ref, a2_ref)
        return carry

    lax.fori_loop(0, S // tk, body, 0)
    o1 = a1_ref[...] / jnp.sum(ls1_ref[...], axis=0, keepdims=True)
    o2 = a2_ref[...] / jnp.sum(ls2_ref[...], axis=0, keepdims=True)
    o = (o1 - _lambda(lam_ref, lam_init) * o2).T
    o = _rms(o, DIFF_DV) * gsub_ref[...] * (1.0 - lam_init)
    o_ref[0] = o.astype(o_ref.dtype)


def _attn_call(q, k, v, extra, bound, *, tq, tk, lam_init=None):
    B, H, S, dqk = q.shape
    dv = v.shape[2]
    tk_safe = min(tk, 512)
    stat = pltpu.VMEM((tq, 1), F32)
    acc = pltpu.VMEM((tq, dv), F32)
    part_t = pltpu.VMEM((8, tq), F32)
    acc_t = pltpu.VMEM((dv, tq), F32)
    qkv_specs = [
        pl.BlockSpec((1, 1, tq, dqk), lambda b, h, i: (b, h, i, 0)),
        pl.BlockSpec((1, 1, S, dqk), lambda b, h, i: (b, h, 0, 0)),
        pl.BlockSpec((1, 1, dv, S), lambda b, h, i: (b, h, 0, 0)),
    ]
    extra_specs = [pl.BlockSpec(e.shape, lambda b, h, i: (0, 0)) for e in extra]
    smem = pl.BlockSpec(memory_space=pltpu.SMEM)
    if lam_init is None:
        safe_kern = functools.partial(_mla_attn_kernel, tk=tk_safe)
        fast_kern = functools.partial(_mla_shifted_kernel, tk=tk)
        safe_scratch, fast_scratch = [stat, stat, acc], [part_t, acc_t]
        name = "mla_attn"
    else:
        safe_kern = functools.partial(_diff_attn_kernel, tk=tk_safe, lam_init=lam_init)
        fast_kern = functools.partial(_diff_shifted_kernel, tk=tk, lam_init=lam_init)
        safe_scratch, fast_scratch = [stat, stat, acc, stat, stat, acc], [part_t, acc_t, part_t, acc_t]
        name = "diff_attn"
    common = dict(
        grid=(B, H, S // tq),
        out_specs=pl.BlockSpec((1, tq, dv), lambda b, h, i: (b, i, h)),
        out_shape=jax.ShapeDtypeStruct((B, S, H * dv), BF16),
        compiler_params=_params(("arbitrary", "arbitrary", "arbitrary")),
    )

    def fast(bound, q, k, v, *extra):
        return pl.pallas_call(fast_kern, in_specs=[smem] + qkv_specs + extra_specs, scratch_shapes=fast_scratch,
                              name=name + "_shifted", **common)(bound.reshape(1), q, k, v, *extra)

    def safe(bound, q, k, v, *extra):
        return pl.pallas_call(safe_kern, in_specs=qkv_specs + extra_specs, scratch_shapes=safe_scratch,
                              name=name, **common)(q, k, v, *extra)

    return lax.cond(bound <= MAX_SHIFT_BOUND, fast, safe, bound, q, k, v, *extra)


def _conv_kernel(prev_ref, cur_ref, next_ref, w_ref, b_ref, g_ref, beta_ref, o_ref, ext_ref, *, rows):
    i = pl.program_id(1)
    ni = pl.num_programs(1)
    tm = cur_ref.shape[1]
    halo = CONV_HALO
    prev = prev_ref[0]
    nxt = next_ref[0]
    ext_ref[0:halo, :] = jnp.where(i > 0, prev, jnp.zeros_like(prev))
    ext_ref[halo:halo + tm, :] = cur_ref[0]
    ext_ref[halo + tm:2 * halo + tm, :] = jnp.where(i < ni - 1, nxt, jnp.zeros_like(nxt))
    w = w_ref[...]
    bias = b_ref[...]
    g = g_ref[...]
    beta = beta_ref[...]
    off = halo - CONV_WIDTH // 2

    for r0 in range(0, tm, rows):
        acc = jnp.zeros((rows, CONV_CH), F32)
        for j in range(CONV_WIDTH):
            acc = acc + ext_ref[r0 + j + off:r0 + j + off + rows, :] * w[j:j + 1, :]
        acc = acc + bias
        mu = jnp.mean(acc, axis=-1, keepdims=True)
        d = acc - mu
        var = jnp.mean(d * d, axis=-1, keepdims=True)
        y = d * lax.rsqrt(var + EPS) * g + beta
        o_ref[0, r0:r0 + rows, :] = (y * jax.nn.sigmoid(y)).astype(o_ref.dtype)


def _conv_call(glu, w, b, g, beta, *, tm, rows=64):
    B, S, C = glu.shape
    hb = tm // CONV_HALO
    nh = S // CONV_HALO
    const = lambda shape: pl.BlockSpec(shape, lambda b_, i: (0, 0))
    return pl.pallas_call(
        functools.partial(_conv_kernel, rows=min(rows, tm)),
        grid=(B, S // tm),
        in_specs=[
            pl.BlockSpec((1, CONV_HALO, C), lambda b_, i: (b_, jnp.maximum(i * hb - 1, 0), 0)),
            pl.BlockSpec((1, tm, C), lambda b_, i: (b_, i, 0)),
            pl.BlockSpec((1, CONV_HALO, C), lambda b_, i: (b_, jnp.minimum((i + 1) * hb, nh - 1), 0)),
            const(w.shape), const((1, C)), const((1, C)), const((1, C)),
        ],
        out_specs=pl.BlockSpec((1, tm, C), lambda b_, i: (b_, i, 0)),
        out_shape=jax.ShapeDtypeStruct((B, S, C), BF16),
        scratch_shapes=[pltpu.VMEM((tm + 2 * CONV_HALO, C), F32)],
        compiler_params=_params(("arbitrary", "arbitrary")),
        name="conv",
    )(glu, glu, glu, w, b, g, beta)


def _out_kernel(x_ref, mod_ref, mla_ref, dif_ref, cnv_ref, w_ref, o_ref):
    n1 = mla_ref.shape[-1]
    n2 = n1 + dif_ref.shape[-1]
    acc = (_dot(mla_ref[0], w_ref[0:n1, :]) + _dot(dif_ref[0], w_ref[n1:n2, :])
           + _dot(cnv_ref[0], w_ref[n2:, :]))
    o_ref[0] = x_ref[0] + mod_ref[0, 5:6, :] * acc


def _out_call(x, mod, o_mla, o_dif, o_cnv, w_out, *, tm):
    B, S, D = x.shape
    tok = lambda w: pl.BlockSpec((1, tm, w), lambda b, i: (b, i, 0))
    return pl.pallas_call(
        _out_kernel,
        grid=(B, S // tm),
        in_specs=[tok(D), pl.BlockSpec((1, N_MOD, D), lambda b, i: (b, 0, 0)),
                  tok(o_mla.shape[-1]), tok(o_dif.shape[-1]), tok(o_cnv.shape[-1]),
                  pl.BlockSpec(w_out.shape, lambda b, i: (0, 0))],
        out_specs=tok(D),
        out_shape=jax.ShapeDtypeStruct((B, S, D), F32),
        compiler_params=_params(("arbitrary", "arbitrary")),
        name="out",
    )(x, mod, o_mla, o_dif, o_cnv, w_out)


def _rope_tables(S):
    half = DIFF_DK // 2
    inv_freq = jnp.float32(ROPE_THETA) ** (-jnp.arange(half, dtype=F32) * (2.0 / DIFF_DK))
    ang = jnp.arange(S, dtype=jnp.int32).astype(F32)[:, None] * inv_freq[None, :]
    c, s, z = jnp.cos(ang), jnp.sin(ang), jnp.zeros_like(ang)
    return (jnp.concatenate([c, c, c, c], axis=1),
            jnp.concatenate([-s, z, -s, z], axis=1),
            jnp.concatenate([z, s, z, s], axis=1))


def _layer_weights(l, w_in, g_cq, g_ckv, w_uq, w_ukv, g_mla_q, g_mla_k, g_diff_q, g_diff_k,
                   lam_q1, lam_k1, lam_q2, lam_k2, g_diff_sub, w_dw, b_dw, g_conv, b_conv, w_out):
    wi = w_in[l]
    o_rope = MLA_Q_RANK + MLA_KV_RANK
    o_dq = o_rope + MLA_ROPE
    w_in_p = jnp.concatenate(
        [wi[:, :o_dq], jnp.zeros((D_MODEL, LANE - MLA_ROPE), wi.dtype), wi[:, o_dq:]], axis=1).astype(BF16)
    w_uq_p = jnp.pad(w_uq[l].reshape(MLA_Q_RANK, MLA_HEADS, MLA_QK),
                     ((0, 0), (0, 0), (0, MLA_QK_PAD - MLA_QK))).reshape(MLA_Q_RANK, -1).astype(BF16)
    pad_qk = lambda g: jnp.pad(g, (0, MLA_QK_PAD - MLA_QK)).reshape(1, MLA_QK_PAD)
    return dict(
        w_in=w_in_p, g_cq=g_cq[l].reshape(1, -1), g_ckv=g_ckv[l].reshape(1, -1),
        w_uq=w_uq_p, w_ukv=w_ukv[l].astype(BF16),
        g_q=pad_qk(g_mla_q[l] * (1.0 / math.sqrt(MLA_QK))), g_k=pad_qk(g_mla_k[l]),
        g_dq=jnp.tile(g_diff_q[l] * (1.0 / math.sqrt(DIFF_DK)), 2 * DIFF_HEADS).reshape(1, -1),
        g_dk=jnp.tile(g_diff_k[l], 2 * DIFF_HEADS).reshape(1, -1),
        bd=jnp.asarray(np.kron(np.eye(2 * DIFF_HEADS), np.ones((DIFF_DK, DIFF_DK))), BF16),
        lam=jnp.stack([lam_q1[l], lam_k1[l], lam_q2[l], lam_k2[l]]),
        b_mla=1.01 * math.sqrt(MLA_QK) * jnp.max(jnp.abs(g_mla_q[l])) * jnp.max(jnp.abs(g_mla_k[l])),
        b_dif=1.01 * math.sqrt(DIFF_DK) * jnp.max(jnp.abs(g_diff_q[l])) * jnp.max(jnp.abs(g_diff_k[l])),
        g_sub=g_diff_sub[l].reshape(1, -1),
        w_dw=jnp.pad(w_dw[l], ((0, 1), (0, 0))), b_dw=b_dw[l].reshape(1, -1),
        g_conv=g_conv[l].reshape(1, -1), b_conv=b_conv[l].reshape(1, -1),
        w_out=w_out[l].astype(BF16),
    )


def _tile(S, t):
    return min(S, t)


def kernel(x_prompt, x_sample, c_prompt, c_sample, w_ada, b_ada, g_norm, w_ffn1_in, w_ffn1_out, w_ffn2_in, w_ffn2_out, w_in, g_cq, g_ckv, w_uq, w_ukv, g_mla_q, g_mla_k, g_diff_q, g_diff_k, lam_q1, lam_k1, lam_q2, lam_k2, g_diff_sub, w_dw, b_dw, g_conv, b_conv, w_out):
    L = w_ada.shape[0]
    bp, bs = c_prompt.shape[0], c_sample.shape[0]
    rows = -(-(bp + bs) // 8) * 8
    c_all = jnp.pad(jnp.concatenate([c_prompt, c_sample], axis=0), ((0, rows - bp - bs), (0, 0)))
    mod_all = _ada_call(c_all, w_ada, b_ada)

    up1, dn1 = w_ffn1_in.astype(BF16), w_ffn1_out.astype(BF16)
    up2, dn2 = w_ffn2_in.astype(BF16), w_ffn2_out.astype(BF16)
    layers = [_layer_weights(l, w_in, g_cq, g_ckv, w_uq, w_ukv, g_mla_q, g_mla_k, g_diff_q, g_diff_k,
                             lam_q1, lam_k1, lam_q2, lam_k2, g_diff_sub, w_dw, b_dw, g_conv, b_conv, w_out)
              for l in range(L)]

    def trunk(x, row0):
        B, S, D = x.shape
        tm = _tile(S, 512)
        tabs = _rope_tables(S)
        for l in range(L):
            lw = layers[l]
            mod = mod_all[l, row0:row0 + B].reshape(B, N_MOD, D)
            gn = g_norm[l]
            x = _ffn_call(x, mod, gn[0:1], up1[l], dn1[l], gn[3:4], sub=0, final_norm=False, tm=tm, tf=512)
            q, k, v, dq, dk, dv, glu = _proj_call(x, mod, gn[1:2], lw, tabs, tm=tm)
            tq = _tile(S, 1024)
            o_mla = _attn_call(q, k, v, (), lw["b_mla"], tq=tq, tk=_tile(S, 2048))
            lam_init = 0.8 - 0.6 * math.exp(-0.3 * l)
            o_dif = _attn_call(dq, dk, dv, (lw["lam"], lw["g_sub"]), lw["b_dif"], tq=tq, tk=_tile(S, 2048),
                               lam_init=lam_init)
            o_cnv = _conv_call(glu, lw["w_dw"], lw["b_dw"], lw["g_conv"], lw["b_conv"], tm=tm)
            x = _out_call(x, mod, o_mla, o_dif, o_cnv, lw["w_out"], tm=tm)
            x = _ffn_call(x, mod, gn[2:3], up2[l], dn2[l], gn[3:4], sub=2, final_norm=True, tm=tm, tf=512)
        return x

    return (trunk(x_prompt, 0), trunk(x_sample, bp))
```

```python
import functools
import math

import numpy as np
import jax
import jax.numpy as jnp
from jax import lax
from jax.experimental import pallas as pl
from jax.experimental.pallas import tpu as pltpu

F32 = jnp.float32
BF16 = jnp.bfloat16

D_MODEL = 2048
MLA_HEADS = 6
MLA_Q_RANK = 512
MLA_KV_RANK = 256
MLA_NOPE = 128
MLA_ROPE = 64
MLA_V = 128
MLA_QK = MLA_NOPE + MLA_ROPE
MLA_QK_PAD = 256
DIFF_HEADS = 6
DIFF_DK = 64
DIFF_DV = 128
CONV_CH = 512
CONV_WIDTH = 31
CONV_HALO = 16
D_FF = 5632
N_MOD = 9
ROPE_THETA = 10000.0
EPS = 1e-6
LANE = 128

SEC_A = (0, 896)
SEC_DQ = (896, 1664)
SEC_DK = (1664, 2432)
SEC_DV = (2432, 3200)
SEC_CU = (3200, 4224)
IN_COLS_PAD = 4224

VMEM_LIMIT = 56 * 1024 * 1024
MAX_SHIFT_BOUND = 40.0
FFN_TF = 512
GROUP_TILE = 256


def _dot(a, b):
    return jnp.dot(a, b, preferred_element_type=F32)


def _dot_nt(a, b):
    return lax.dot_general(a, b, (((1,), (1,)), ((), ())), preferred_element_type=F32)


def _rms(x, width):
    return x * lax.rsqrt(jnp.sum(x * x, axis=-1, keepdims=True) * (1.0 / width) + EPS)


def _adaln(x, g, sc, sh):
    return (_rms(x, x.shape[-1]) * g) * (1.0 + sc) + sh


def _rope(v, cos, sin_lo, sin_hi):
    return v * cos + pltpu.roll(v, 96, 1) * sin_lo + pltpu.roll(v, 32, 1) * sin_hi


def _params(sem, vmem=VMEM_LIMIT):
    return pltpu.CompilerParams(dimension_semantics=sem, vmem_limit_bytes=vmem)


def _ada_kernel(c_ref, w_ref, b_ref, o_ref):
    c = c_ref[...]
    s = c * jax.nn.sigmoid(c)
    s_hi = s.astype(BF16)
    s_lo = (s - s_hi.astype(F32)).astype(BF16)
    w = w_ref[0]
    w_hi = w.astype(BF16)
    w_lo = (w - w_hi.astype(F32)).astype(BF16)
    o_ref[0] = _dot(s_hi, w_hi) + _dot(s_lo, w_hi) + _dot(s_hi, w_lo) + b_ref[0]


def _ada_call(c_all, w_ada, b_ada, tn=512):
    L, D, N = w_ada.shape
    R = c_all.shape[0]
    return pl.pallas_call(
        _ada_kernel,
        grid=(L, N // tn),
        in_specs=[
            pl.BlockSpec((R, D), lambda l, n: (0, 0)),
            pl.BlockSpec((1, D, tn), lambda l, n: (l, 0, n)),
            pl.BlockSpec((1, 1, tn), lambda l, n: (l, 0, n)),
        ],
        out_specs=pl.BlockSpec((1, R, tn), lambda l, n: (l, 0, n)),
        out_shape=jax.ShapeDtypeStruct((L, R, N), F32),
        compiler_params=_params(("arbitrary", "arbitrary")),
        name="ada",
    )(c_all, w_ada, b_ada.reshape(L, 1, N))


def _ffn_kernel(x_ref, mod_ref, g_ref, wa_ref, wb_ref, wd_ref, gf_ref, o_ref, h_ref, act_ref, *, sub, final_norm):
    j = pl.program_id(2)
    nf = pl.num_programs(2) - 1

    def up(slot):
        h = h_ref[...]
        a = _dot(h, wa_ref[...])
        b = _dot(h, wb_ref[...])
        act_ref[slot] = (a * jax.nn.sigmoid(a) * b).astype(BF16)

    def down(slot):
        return _dot(act_ref[slot], wd_ref[...])

    @pl.when(j == 0)
    def _():
        sh = mod_ref[0, 3 * sub:3 * sub + 1, :]
        sc = mod_ref[0, 3 * sub + 1:3 * sub + 2, :]
        h_ref[...] = _adaln(x_ref[0], g_ref[...], sc, sh).astype(BF16)
        o_ref[0] = jnp.zeros(o_ref.shape[1:], F32)
        up(0)

    @pl.when((j > 0) & (j < nf))
    def _():
        up(j % 2)
        o_ref[0] += down((j - 1) % 2)

    @pl.when(j == nf)
    def _():
        ga = mod_ref[0, 3 * sub + 2:3 * sub + 3, :]
        y = x_ref[0] + 0.5 * ga * (o_ref[0] + down((nf - 1) % 2))
        if final_norm:
            y = _rms(y, y.shape[-1]) * gf_ref[...]
        o_ref[0] = y


def _ffn_call(x, mod, g_pre, w_up, w_down, g_fin, *, sub, final_norm, tm):
    B, S, D = x.shape
    tf = FFN_TF
    nf = D_FF // tf
    kern = functools.partial(_ffn_kernel, sub=sub, final_norm=final_norm)
    return pl.pallas_call(
        kern,
        grid=(B, S // tm, nf + 1),
        in_specs=[
            pl.BlockSpec((1, tm, D), lambda b, i, j: (b, i, 0)),
            pl.BlockSpec((1, N_MOD, D), lambda b, i, j: (b, 0, 0)),
            pl.BlockSpec((1, D), lambda b, i, j: (0, 0)),
            pl.BlockSpec((None, D, tf), lambda b, i, j: (jnp.minimum(j, nf - 1), 0, 0)),
            pl.BlockSpec((None, D, tf), lambda b, i, j: (jnp.minimum(j, nf - 1) + nf, 0, 0)),
            pl.BlockSpec((tf, D), lambda b, i, j: (jnp.maximum(j - 1, 0), 0)),
            pl.BlockSpec((1, D), lambda b, i, j: (0, 0)),
        ],
        out_specs=pl.BlockSpec((1, tm, D), lambda b, i, j: (b, i, 0)),
        out_shape=jax.ShapeDtypeStruct((B, S, D), F32),
        scratch_shapes=[pltpu.VMEM((tm, D), BF16), pltpu.VMEM((2, tm, tf), BF16)],
        compiler_params=_params(("arbitrary", "arbitrary", "arbitrary")),
        name="ffn",
    )(x, mod, g_pre, w_up, w_up, w_down, g_fin)


def _proj_kernel(x_ref, mod_ref, g_ref, win_ref, gcq_ref, gckv_ref, wuq_ref, wukv_ref, gq_ref, gk_ref,
                 gdq_ref, gdk_ref, bd_ref, cos_ref, slo_ref, shi_ref,
                 q_ref, k_ref, v_ref, dq_ref, dk_ref, dv_ref, glu_ref):
    sh = mod_ref[0, 3:4, :]
    sc = mod_ref[0, 4:5, :]
    h = _adaln(x_ref[0], g_ref[...], sc, sh).astype(BF16)
    cos = cos_ref[...]
    slo = slo_ref[...]
    shi = shi_ref[...]

    pa = _dot(h, win_ref[:, SEC_A[0]:SEC_A[1]])
    cq = pa[:, 0:MLA_Q_RANK]
    ckv = pa[:, MLA_Q_RANK:MLA_Q_RANK + MLA_KV_RANK]
    kr = pa[:, MLA_Q_RANK + MLA_KV_RANK:SEC_A[1]]
    cqn = (_rms(cq, MLA_Q_RANK) * gcq_ref[...]).astype(BF16)
    ckvn = (_rms(ckv, MLA_KV_RANK) * gckv_ref[...]).astype(BF16)
    qraw = _dot(cqn, wuq_ref[...])
    kv = _dot(ckvn, wukv_ref[...])
    kr_ss = jnp.sum(kr * kr, axis=-1, keepdims=True)
    gq = gq_ref[...]
    gk = gk_ref[...]
    for hd in range(MLA_HEADS):
        c0 = hd * MLA_QK_PAD
        qh = qraw[:, c0:c0 + MLA_QK_PAD]
        qn = _rms(qh, MLA_QK) * gq
        q_ref[0, hd, :, 0:LANE] = qn[:, 0:LANE].astype(BF16)
        q_ref[0, hd, :, LANE:2 * LANE] = _rope(qn[:, LANE:2 * LANE], cos, slo, shi).astype(BF16)
        kn = kv[:, c0:c0 + LANE]
        inv = lax.rsqrt((jnp.sum(kn * kn, axis=-1, keepdims=True) + kr_ss) * (1.0 / MLA_QK) + EPS)
        k_ref[0, hd, :, 0:LANE] = (kn * inv * gk[:, 0:LANE]).astype(BF16)
        k_ref[0, hd, :, LANE:2 * LANE] = _rope(kr * inv * gk[:, LANE:2 * LANE], cos, slo, shi).astype(BF16)
        v_ref[0, hd] = kv[:, c0 + LANE:c0 + 2 * LANE].T.astype(BF16)

    bd = bd_ref[...]
    for sec, g_t, o_ref in ((SEC_DQ, gdq_ref, dq_ref), (SEC_DK, gdk_ref, dk_ref)):
        z = _dot(h, win_ref[:, sec[0]:sec[1]])
        g_all = g_t[...]
        for c0 in range(0, z.shape[1], GROUP_TILE):
            zc = z[:, c0:c0 + GROUP_TILE]
            zz = zc * zc
            zz_hi = zz.astype(BF16)
            zz_lo = (zz - zz_hi.astype(F32)).astype(BF16)
            ss = _dot(zz_hi, bd) + _dot(zz_lo, bd)
            zn = zc * lax.rsqrt(ss * (1.0 / DIFF_DK) + EPS) * g_all[:, c0:c0 + GROUP_TILE]
            for t in range(GROUP_TILE // LANE):
                o_ref[0, c0 // LANE + t] = _rope(zn[:, t * LANE:(t + 1) * LANE], cos, slo, shi).astype(BF16)
    z = _dot(h, win_ref[:, SEC_DV[0]:SEC_DV[1]])
    for hd in range(DIFF_HEADS):
        dv_ref[0, hd] = z[:, hd * LANE:(hd + 1) * LANE].T.astype(BF16)

    cu = _dot(h, win_ref[:, SEC_CU[0]:SEC_CU[1]])
    glu_ref[0] = cu[:, 0:CONV_CH] * jax.nn.sigmoid(cu[:, CONV_CH:2 * CONV_CH])


def _proj_call(x, mod, g_pre, lw, rope_tabs, *, tm):
    B, S, D = x.shape
    H = MLA_HEADS
    const = lambda shape: pl.BlockSpec(shape, lambda b, i: (0,) * len(shape))
    tab = pl.BlockSpec((tm, LANE), lambda b, i: (i, 0))
    head_out = lambda w: pl.BlockSpec((1, H, tm, w), lambda b, i: (b, 0, i, 0))
    head_shape = lambda w: jax.ShapeDtypeStruct((B, H, S, w), BF16)
    val_out = pl.BlockSpec((1, H, LANE, tm), lambda b, i: (b, 0, 0, i))
    val_shape = jax.ShapeDtypeStruct((B, H, LANE, S), BF16)
    return pl.pallas_call(
        _proj_kernel,
        grid=(B, S // tm),
        in_specs=[
            pl.BlockSpec((1, tm, D), lambda b, i: (b, i, 0)),
            pl.BlockSpec((1, N_MOD, D), lambda b, i: (b, 0, 0)),
            const((1, D)),
            const((D, IN_COLS_PAD)),
            const((1, MLA_Q_RANK)),
            const((1, MLA_KV_RANK)),
            const((MLA_Q_RANK, H * MLA_QK_PAD)),
            const((MLA_KV_RANK, H * MLA_QK_PAD)),
            const((1, MLA_QK_PAD)),
            const((1, MLA_QK_PAD)),
            const((1, DIFF_HEADS * LANE)),
            const((1, DIFF_HEADS * LANE)),
            const((GROUP_TILE, GROUP_TILE)),
            tab, tab, tab,
        ],
        out_specs=[head_out(MLA_QK_PAD), head_out(MLA_QK_PAD), val_out,
                   head_out(LANE), head_out(LANE), val_out,
                   pl.BlockSpec((1, tm, CONV_CH), lambda b, i: (b, i, 0))],
        out_shape=[head_shape(MLA_QK_PAD), head_shape(MLA_QK_PAD), val_shape,
                   head_shape(LANE), head_shape(LANE), val_shape,
                   jax.ShapeDtypeStruct((B, S, CONV_CH), F32)],
        compiler_params=_params(("arbitrary", "arbitrary")),
        name="proj",
    )(x, mod, g_pre, lw["w_in"], lw["g_cq"], lw["g_ckv"], lw["w_uq"], lw["w_ukv"], lw["g_q"], lw["g_k"],
      lw["g_dq"], lw["g_dk"], lw["bd"], *rope_tabs)


def _softmax_step(s, vt, m_ref, l_ref, acc_ref):
    m_prev = m_ref[...]
    m_new = jnp.maximum(m_prev, jnp.max(s, axis=-1, keepdims=True))
    alpha = jnp.exp(m_prev - m_new)
    p = jnp.exp(s - m_new)
    l_ref[...] = alpha * l_ref[...] + jnp.sum(p, axis=-1, keepdims=True)
    acc_ref[...] = alpha * acc_ref[...] + _dot_nt(p.astype(BF16), vt)
    m_ref[...] = m_new


def _split_maps(q):
    lane = lax.broadcasted_iota(jnp.int32, q.shape, 1)
    zero = jnp.zeros_like(q)
    return jnp.where(lane < DIFF_DK, q, zero), jnp.where(lane >= DIFF_DK, q, zero)


def _lambda(lam_ref, lam_init):
    lv = lam_ref[...]
    return (jnp.exp(jnp.sum(lv[0:1] * lv[1:2], axis=-1, keepdims=True))
            - jnp.exp(jnp.sum(lv[2:3] * lv[3:4], axis=-1, keepdims=True)) + lam_init)


def _mla_attn_kernel(q_ref, k_ref, v_ref, o_ref, m_ref, l_ref, acc_ref, *, tk):
    S = k_ref.shape[2]
    q = q_ref[0, 0]
    m_ref[...] = jnp.full(m_ref.shape, -jnp.inf, F32)
    l_ref[...] = jnp.zeros(l_ref.shape, F32)
    acc_ref[...] = jnp.zeros(acc_ref.shape, F32)

    def body(c, carry):
        r0 = pl.multiple_of(c * tk, tk)
        s = _dot_nt(q, k_ref[0, 0, pl.ds(r0, tk), :])
        _softmax_step(s, v_ref[0, 0, :, pl.ds(r0, tk)], m_ref, l_ref, acc_ref)
        return carry

    lax.fori_loop(0, S // tk, body, 0)
    o_ref[0] = (acc_ref[...] / l_ref[...]).astype(o_ref.dtype)


def _diff_attn_kernel(q_ref, k_ref, v_ref, lam_ref, gsub_ref, o_ref,
                      m1_ref, l1_ref, a1_ref, m2_ref, l2_ref, a2_ref, *, tk, lam_init):
    S = k_ref.shape[2]
    q1, q2 = _split_maps(q_ref[0, 0])
    for m_ref, l_ref, a_ref in ((m1_ref, l1_ref, a1_ref), (m2_ref, l2_ref, a2_ref)):
        m_ref[...] = jnp.full(m_ref.shape, -jnp.inf, F32)
        l_ref[...] = jnp.zeros(l_ref.shape, F32)
        a_ref[...] = jnp.zeros(a_ref.shape, F32)

    def body(c, carry):
        r0 = pl.multiple_of(c * tk, tk)
        ks = k_ref[0, 0, pl.ds(r0, tk), :]
        vt = v_ref[0, 0, :, pl.ds(r0, tk)]
        _softmax_step(_dot_nt(q1, ks), vt, m1_ref, l1_ref, a1_ref)
        _softmax_step(_dot_nt(q2, ks), vt, m2_ref, l2_ref, a2_ref)
        return carry

    lax.fori_loop(0, S // tk, body, 0)
    o = a1_ref[...] / l1_ref[...] - _lambda(lam_ref, lam_init) * (a2_ref[...] / l2_ref[...])
    o = _rms(o, DIFF_DV) * gsub_ref[...] * (1.0 - lam_init)
    o_ref[0] = o.astype(o_ref.dtype)


def _shifted_step(q, ks, vt, bound, ls_ref, acc_ref):
    pt = jnp.exp(_dot_nt(ks, q) - bound)
    part = pt[0:8, :]
    for t in range(1, pt.shape[0] // 8):
        part = part + pt[8 * t:8 * t + 8, :]
    ls_ref[...] += part
    acc_ref[...] += _dot(vt, pt.astype(BF16))


def _mla_shifted_kernel(b_ref, q_ref, k_ref, v_ref, o_ref, ls_ref, acc_ref, *, tk):
    S = k_ref.shape[2]
    q = q_ref[0, 0]
    bound = b_ref[0]
    ls_ref[...] = jnp.zeros(ls_ref.shape, F32)
    acc_ref[...] = jnp.zeros(acc_ref.shape, F32)

    def body(c, carry):
        r0 = pl.multiple_of(c * tk, tk)
        _shifted_step(q, k_ref[0, 0, pl.ds(r0, tk), :], v_ref[0, 0, :, pl.ds(r0, tk)], bound, ls_ref, acc_ref)
        return carry

    lax.fori_loop(0, S // tk, body, 0, unroll=2)
    ot = acc_ref[...] / jnp.sum(ls_ref[...], axis=0, keepdims=True)
    o_ref[0] = ot.T.astype(o_ref.dtype)


def _diff_shifted_kernel(b_ref, q_ref, k_ref, v_ref, lam_ref, gsub_ref, o_ref,
                         ls1_ref, a1_ref, ls2_ref, a2_ref, *, tk, lam_init):
    S = k_ref.shape[2]
    q1, q2 = _split_maps(q_ref[0, 0])
    bound = b_ref[0]
    for r in (ls1_ref, a1_ref, ls2_ref, a2_ref):
        r[...] = jnp.zeros(r.shape, F32)

    def body(c, carry):
        r0 = pl.multiple_of(c * tk, tk)
        ks = k_ref[0, 0, pl.ds(r0, tk), :]
        vt = v_ref[0, 0, :, pl.ds(r0, tk)]
        _shifted_step(q1, ks, vt, bound, ls1_ref, a1_ref)
        _shifted_step(q2, ks, vt, bound, ls2_ref, a2_ref)
        return carry

    lax.fori_loop(0, S // tk, body, 0, unroll=2)
    o1 = a1_ref[...] / jnp.sum(ls1_ref[...], axis=0, keepdims=True)
    o2 = a2_ref[...] / jnp.sum(ls2_ref[...], axis=0, keepdims=True)
    o = (o1 - _lambda(lam_ref, lam_init) * o2).T
    o = _rms(o, DIFF_DV) * gsub_ref[...] * (1.0 - lam_init)
    o_ref[0] = o.astype(o_ref.dtype)


def _attn_call(q, k, v, extra, bound, *, tq, tk, lam_init=None):
    B, H, S, dqk = q.shape
    dv = v.shape[2]
    tk_safe = min(tk, 512)
    stat = pltpu.VMEM((tq, 1), F32)
    acc = pltpu.VMEM((tq, dv), F32)
    part_t = pltpu.VMEM((8, tq), F32)
    acc_t = pltpu.VMEM((dv, tq), F32)
    qkv_specs = [
        pl.BlockSpec((1, 1, tq, dqk), lambda b, h, i: (b, h, i, 0)),
        pl.BlockSpec((1, 1, S, dqk), lambda b, h, i: (b, h, 0, 0)),
        pl.BlockSpec((1, 1, dv, S), lambda b, h, i: (b, h, 0, 0)),
    ]
    extra_specs = [pl.BlockSpec(e.shape, lambda b, h, i: (0, 0)) for e in extra]
    smem = pl.BlockSpec(memory_space=pltpu.SMEM)
    if lam_init is None:
        safe_kern = functools.partial(_mla_attn_kernel, tk=tk_safe)
        fast_kern = functools.partial(_mla_shifted_kernel, tk=tk)
        safe_scratch, fast_scratch = [stat, stat, acc], [part_t, acc_t]
        name = "mla_attn"
    else:
        safe_kern = functools.partial(_diff_attn_kernel, tk=tk_safe, lam_init=lam_init)
        fast_kern = functools.partial(_diff_shifted_kernel, tk=tk, lam_init=lam_init)
        safe_scratch, fast_scratch = [stat, stat, acc, stat, stat, acc], [part_t, acc_t, part_t, acc_t]
        name = "diff_attn"
    common = dict(
        grid=(B, H, S // tq),
        out_specs=pl.BlockSpec((1, tq, dv), lambda b, h, i: (b, i, h)),
        out_shape=jax.ShapeDtypeStruct((B, S, H * dv), BF16),
        compiler_params=_params(("arbitrary", "arbitrary", "arbitrary")),
    )

    def fast(bound, q, k, v, *extra):
        return pl.pallas_call(fast_kern, in_specs=[smem] + qkv_specs + extra_specs, scratch_shapes=fast_scratch,
                              name=name + "_shifted", **common)(bound.reshape(1), q, k, v, *extra)

    def safe(bound, q, k, v, *extra):
        return pl.pallas_call(safe_kern, in_specs=qkv_specs + extra_specs, scratch_shapes=safe_scratch,
                              name=name, **common)(q, k, v, *extra)

    return lax.cond(bound <= MAX_SHIFT_BOUND, fast, safe, bound, q, k, v, *extra)


def _conv_kernel(prev_ref, cur_ref, next_ref, w_ref, b_ref, g_ref, beta_ref, o_ref, ext_ref, *, rows):
    i = pl.program_id(1)
    ni = pl.num_programs(1)
    tm = cur_ref.shape[1]
    halo = CONV_HALO
    prev = prev_ref[0]
    nxt = next_ref[0]
    ext_ref[0, 0:halo, :] = jnp.where(i > 0, prev, jnp.zeros_like(prev))
    ext_ref[0, halo:halo + tm, :] = cur_ref[0]
    ext_ref[0, halo + tm:2 * halo + tm, :] = jnp.where(i < ni - 1, nxt, jnp.zeros_like(nxt))
    off = halo - CONV_WIDTH // 2
    span = tm + 8 * ((off + CONV_WIDTH - 1) // 8)
    for s in range(1, 8):
        for c0 in range(0, span, rows):
            n = min(rows, span - c0)
            ext_ref[s, c0:c0 + n, :] = ext_ref[0, c0 + s:c0 + s + n, :]
    w = w_ref[...]
    bias = b_ref[...]
    g = g_ref[...]
    beta = beta_ref[...]

    def body(r, carry):
        r0 = pl.multiple_of(r * rows, rows)
        acc = jnp.zeros((rows, CONV_CH), F32)
        for j in range(CONV_WIDTH):
            a, s = divmod(j + off, 8)
            acc = acc + ext_ref[s, pl.ds(r0 + 8 * a, rows), :] * w[j:j + 1, :]
        acc = acc + bias
        mu = jnp.mean(acc, axis=-1, keepdims=True)
        d = acc - mu
        var = jnp.mean(d * d, axis=-1, keepdims=True)
        y = d * lax.rsqrt(var + EPS) * g + beta
        o_ref[0, pl.ds(r0, rows), :] = (y * jax.nn.sigmoid(y)).astype(o_ref.dtype)
        return carry

    lax.fori_loop(0, tm // rows, body, 0)


def _conv_call(glu, w, b, g, beta, *, tm, rows=64):
    B, S, C = glu.shape
    hb = tm // CONV_HALO
    nh = S // CONV_HALO
    const = lambda shape: pl.BlockSpec(shape, lambda b_, i: (0, 0))
    return pl.pallas_call(
        functools.partial(_conv_kernel, rows=min(rows, tm)),
        grid=(B, S // tm),
        in_specs=[
            pl.BlockSpec((1, CONV_HALO, C), lambda b_, i: (b_, jnp.maximum(i * hb - 1, 0), 0)),
            pl.BlockSpec((1, tm, C), lambda b_, i: (b_, i, 0)),
            pl.BlockSpec((1, CONV_HALO, C), lambda b_, i: (b_, jnp.minimum((i + 1) * hb, nh - 1), 0)),
            const(w.shape), const((1, C)), const((1, C)), const((1, C)),
        ],
        out_specs=pl.BlockSpec((1, tm, C), lambda b_, i: (b_, i, 0)),
        out_shape=jax.ShapeDtypeStruct((B, S, C), BF16),
        scratch_shapes=[pltpu.VMEM((8, tm + 2 * CONV_HALO, C), F32)],
        compiler_params=_params(("arbitrary", "arbitrary")),
        name="conv",
    )(glu, glu, glu, w, b, g, beta)


def _out_kernel(x_ref, mod_ref, mla_ref, dif_ref, cnv_ref, w_ref, o_ref):
    n1 = mla_ref.shape[-1]
    n2 = n1 + dif_ref.shape[-1]
    acc = (_dot(mla_ref[0], w_ref[0:n1, :]) + _dot(dif_ref[0], w_ref[n1:n2, :])
           + _dot(cnv_ref[0], w_ref[n2:, :]))
    o_ref[0] = x_ref[0] + mod_ref[0, 5:6, :] * acc


def _out_call(x, mod, o_mla, o_dif, o_cnv, w_out, *, tm):
    B, S, D = x.shape
    tok = lambda w: pl.BlockSpec((1, tm, w), lambda b, i: (b, i, 0))
    return pl.pallas_call(
        _out_kernel,
        grid=(B, S // tm),
        in_specs=[tok(D), pl.BlockSpec((1, N_MOD, D), lambda b, i: (b, 0, 0)),
                  tok(o_mla.shape[-1]), tok(o_dif.shape[-1]), tok(o_cnv.shape[-1]),
                  pl.BlockSpec(w_out.shape, lambda b, i: (0, 0))],
        out_specs=tok(D),
        out_shape=jax.ShapeDtypeStruct((B, S, D), F32),
        compiler_params=_params(("arbitrary", "arbitrary")),
        name="out",
    )(x, mod, o_mla, o_dif, o_cnv, w_out)


def _rope_tables(S):
    half = DIFF_DK // 2
    inv_freq = jnp.float32(ROPE_THETA) ** (-jnp.arange(half, dtype=F32) * (2.0 / DIFF_DK))
    ang = jnp.arange(S, dtype=jnp.int32).astype(F32)[:, None] * inv_freq[None, :]
    c, s, z = jnp.cos(ang), jnp.sin(ang), jnp.zeros_like(ang)
    return (jnp.concatenate([c, c, c, c], axis=1),
            jnp.concatenate([-s, z, -s, z], axis=1),
            jnp.concatenate([z, s, z, s], axis=1))


def _layer_weights(l, w_in, g_cq, g_ckv, w_uq, w_ukv, g_mla_q, g_mla_k, g_diff_q, g_diff_k,
                   lam_q1, lam_k1, lam_q2, lam_k2, g_diff_sub, w_dw, b_dw, g_conv, b_conv, w_out):
    wi = w_in[l]
    o_rope = MLA_Q_RANK + MLA_KV_RANK
    o_dq = o_rope + MLA_ROPE
    w_in_p = jnp.concatenate(
        [wi[:, :o_dq], jnp.zeros((D_MODEL, LANE - MLA_ROPE), wi.dtype), wi[:, o_dq:]], axis=1).astype(BF16)
    w_uq_p = jnp.pad(w_uq[l].reshape(MLA_Q_RANK, MLA_HEADS, MLA_QK),
                     ((0, 0), (0, 0), (0, MLA_QK_PAD - MLA_QK))).reshape(MLA_Q_RANK, -1).astype(BF16)
    pad_qk = lambda g: jnp.pad(g, (0, MLA_QK_PAD - MLA_QK)).reshape(1, MLA_QK_PAD)
    return dict(
        w_in=w_in_p, g_cq=g_cq[l].reshape(1, -1), g_ckv=g_ckv[l].reshape(1, -1),
        w_uq=w_uq_p, w_ukv=w_ukv[l].astype(BF16),
        g_q=pad_qk(g_mla_q[l] * (1.0 / math.sqrt(MLA_QK))), g_k=pad_qk(g_mla_k[l]),
        g_dq=jnp.tile(g_diff_q[l] * (1.0 / math.sqrt(DIFF_DK)), 2 * DIFF_HEADS).reshape(1, -1),
        g_dk=jnp.tile(g_diff_k[l], 2 * DIFF_HEADS).reshape(1, -1),
        bd=jnp.asarray(np.kron(np.eye(GROUP_TILE // DIFF_DK), np.ones((DIFF_DK, DIFF_DK))), BF16),
        lam=jnp.stack([lam_q1[l], lam_k1[l], lam_q2[l], lam_k2[l]]),
        b_mla=1.01 * math.sqrt(MLA_QK) * jnp.max(jnp.abs(g_mla_q[l])) * jnp.max(jnp.abs(g_mla_k[l])),
        b_dif=1.01 * math.sqrt(DIFF_DK) * jnp.max(jnp.abs(g_diff_q[l])) * jnp.max(jnp.abs(g_diff_k[l])),
        g_sub=g_diff_sub[l].reshape(1, -1),
        w_dw=jnp.pad(w_dw[l], ((0, 1), (0, 0))), b_dw=b_dw[l].reshape(1, -1),
        g_conv=g_conv[l].reshape(1, -1), b_conv=b_conv[l].reshape(1, -1),
        w_out=w_out[l].astype(BF16),
    )


def _tile(S, t):
    return min(S, t)


def kernel(x_prompt, x_sample, c_prompt, c_sample, w_ada, b_ada, g_norm, w_ffn1_in, w_ffn1_out, w_ffn2_in, w_ffn2_out, w_in, g_cq, g_ckv, w_uq, w_ukv, g_mla_q, g_mla_k, g_diff_q, g_diff_k, lam_q1, lam_k1, lam_q2, lam_k2, g_diff_sub, w_dw, b_dw, g_conv, b_conv, w_out):
    L = w_ada.shape[0]
    bp, bs = c_prompt.shape[0], c_sample.shape[0]
    rows = -(-(bp + bs) // 8) * 8
    c_all = jnp.pad(jnp.concatenate([c_prompt, c_sample], axis=0), ((0, rows - bp - bs), (0, 0)))
    mod_all = _ada_call(c_all, w_ada, b_ada)

    def chunked_up(w):
        return w.astype(BF16).reshape(L, D_MODEL, 2 * D_FF // FFN_TF, FFN_TF).transpose(0, 2, 1, 3)

    up1, dn1 = chunked_up(w_ffn1_in), w_ffn1_out.astype(BF16)
    up2, dn2 = chunked_up(w_ffn2_in), w_ffn2_out.astype(BF16)
    layers = [_layer_weights(l, w_in, g_cq, g_ckv, w_uq, w_ukv, g_mla_q, g_mla_k, g_diff_q, g_diff_k,
                             lam_q1, lam_k1, lam_q2, lam_k2, g_diff_sub, w_dw, b_dw, g_conv, b_conv, w_out)
              for l in range(L)]

    def trunk(x, row0):
        B, S, D = x.shape
        tm = _tile(S, 512)
        tabs = _rope_tables(S)
        for l in range(L):
            lw = layers[l]
            mod = mod_all[l, row0:row0 + B].reshape(B, N_MOD, D)
            gn = g_norm[l]
            x = _ffn_call(x, mod, gn[0:1], up1[l], dn1[l], gn[3:4], sub=0, final_norm=False, tm=tm)
            q, k, v, dq, dk, dv, glu = _proj_call(x, mod, gn[1:2], lw, tabs, tm=tm)
            tq = _tile(S, 1024)
            o_mla = _attn_call(q, k, v, (), lw["b_mla"], tq=tq, tk=_tile(S, 2048))
            lam_init = 0.8 - 0.6 * math.exp(-0.3 * l)
            o_dif = _attn_call(dq, dk, dv, (lw["lam"], lw["g_sub"]), lw["b_dif"], tq=tq, tk=_tile(S, 2048),
                               lam_init=lam_init)
            o_cnv = _conv_call(glu, lw["w_dw"], lw["b_dw"], lw["g_conv"], lw["b_conv"], tm=tm)
            x = _out_call(x, mod, o_mla, o_dif, o_cnv, lw["w_out"], tm=tm)
            x = _ffn_call(x, mod, gn[2:3], up2[l], dn2[l], gn[3:4], sub=2, final_norm=True, tm=tm)
        return x

    return (trunk(x_prompt, 0), trunk(x_sample, bp))
```

```python
import functools
import math

import numpy as np
import jax
import jax.numpy as jnp
from jax import lax
from jax.experimental import pallas as pl
from jax.experimental.pallas import tpu as pltpu

F32 = jnp.float32
BF16 = jnp.bfloat16

D_MODEL = 2048
MLA_HEADS = 6
MLA_Q_RANK = 512
MLA_KV_RANK = 256
MLA_NOPE = 128
MLA_ROPE = 64
MLA_V = 128
MLA_QK = MLA_NOPE + MLA_ROPE
MLA_QK_PAD = 256
DIFF_HEADS = 6
DIFF_DK = 64
DIFF_DV = 128
CONV_CH = 512
CONV_WIDTH = 31
CONV_HALO = 16
D_FF = 5632
N_MOD = 9
ROPE_THETA = 10000.0
EPS = 1e-6
LANE = 128

SEC_A = (0, 896)
SEC_DQ = (896, 1664)
SEC_DK = (1664, 2432)
SEC_DV = (2432, 3200)
SEC_CU = (3200, 4224)
IN_COLS_PAD = 4224

VMEM_LIMIT = 56 * 1024 * 1024
FFN_VMEM_LIMIT = 60 * 1024 * 1024
MAX_SHIFT_BOUND = 40.0
LOG2E = 1.4426950408889634
TOKEN_TILE = 512
FFN_TOKEN_TILE = 1024
ATTN_Q_TILE = 1024
ATTN_K_TILE = 2048
FFN_TF = 512
GROUP_TILE = 256


def _dot(a, b):
    return jnp.dot(a, b, preferred_element_type=F32)


def _dot_nt(a, b):
    return lax.dot_general(a, b, (((1,), (1,)), ((), ())), preferred_element_type=F32)


def _rms(x, width):
    return x * lax.rsqrt(jnp.sum(x * x, axis=-1, keepdims=True) * (1.0 / width) + EPS)


def _adaln(x, g, sc, sh):
    return (_rms(x, x.shape[-1]) * g) * (1.0 + sc) + sh


def _rope(v, cos, sin_lo, sin_hi):
    return v * cos + pltpu.roll(v, 96, 1) * sin_lo + pltpu.roll(v, 32, 1) * sin_hi


def _params(sem, vmem=VMEM_LIMIT):
    return pltpu.CompilerParams(dimension_semantics=sem, vmem_limit_bytes=vmem)


def _ada_kernel(c_ref, w_ref, b_ref, o_ref):
    c = c_ref[...]
    s = c * jax.nn.sigmoid(c)
    s_hi = s.astype(BF16)
    s_lo = (s - s_hi.astype(F32)).astype(BF16)
    w = w_ref[0]
    w_hi = w.astype(BF16)
    w_lo = (w - w_hi.astype(F32)).astype(BF16)
    o_ref[0] = _dot(s_hi, w_hi) + _dot(s_lo, w_hi) + _dot(s_hi, w_lo) + b_ref[0]


def _ada_call(c_all, w_ada, b_ada, tn=512):
    L, D, N = w_ada.shape
    R = c_all.shape[0]
    return pl.pallas_call(
        _ada_kernel,
        grid=(L, N // tn),
        in_specs=[
            pl.BlockSpec((R, D), lambda l, n: (0, 0)),
            pl.BlockSpec((1, D, tn), lambda l, n: (l, 0, n)),
            pl.BlockSpec((1, 1, tn), lambda l, n: (l, 0, n)),
        ],
        out_specs=pl.BlockSpec((1, R, tn), lambda l, n: (l, 0, n)),
        out_shape=jax.ShapeDtypeStruct((L, R, N), F32),
        compiler_params=_params(("arbitrary", "arbitrary")),
        name="ada",
    )(c_all, w_ada, b_ada.reshape(L, 1, N))


def _ffn_kernel(x_ref, mod_ref, g_ref, wa_ref, wb_ref, wd_ref, gf_ref, o_ref, h_ref, act_ref, *, sub, final_norm):
    j = pl.program_id(2)
    nf = pl.num_programs(2) - 1

    def up(slot):
        h = h_ref[...]
        a = _dot(h, wa_ref[...])
        b = _dot(h, wb_ref[...])
        act_ref[slot] = (a * jax.nn.sigmoid(a) * b).astype(BF16)

    def down(slot):
        return _dot(act_ref[slot], wd_ref[...])

    @pl.when(j == 0)
    def _():
        sh = mod_ref[0, 3 * sub:3 * sub + 1, :]
        sc = mod_ref[0, 3 * sub + 1:3 * sub + 2, :]
        h_ref[...] = _adaln(x_ref[0], g_ref[...], sc, sh).astype(BF16)
        o_ref[0] = jnp.zeros(o_ref.shape[1:], F32)
        up(0)

    @pl.when((j > 0) & (j < nf))
    def _():
        up(j % 2)
        o_ref[0] += down((j - 1) % 2)

    @pl.when(j == nf)
    def _():
        ga = mod_ref[0, 3 * sub + 2:3 * sub + 3, :]
        y = x_ref[0] + 0.5 * ga * (o_ref[0] + down((nf - 1) % 2))
        if final_norm:
            y = _rms(y, y.shape[-1]) * gf_ref[...]
        o_ref[0] = y


def _ffn_call(x, mod, g_pre, w_up, w_down, g_fin, *, sub, final_norm, tm):
    B, S, D = x.shape
    tf = FFN_TF
    nf = D_FF // tf
    kern = functools.partial(_ffn_kernel, sub=sub, final_norm=final_norm)
    return pl.pallas_call(
        kern,
        grid=(B, S // tm, nf + 1),
        in_specs=[
            pl.BlockSpec((1, tm, D), lambda b, i, j: (b, i, 0)),
            pl.BlockSpec((1, N_MOD, D), lambda b, i, j: (b, 0, 0)),
            pl.BlockSpec((1, D), lambda b, i, j: (0, 0)),
            pl.BlockSpec((None, D, tf), lambda b, i, j: (jnp.where(j < nf, j, 0), 0, 0)),
            pl.BlockSpec((None, D, tf), lambda b, i, j: (jnp.where(j < nf, j, 0) + nf, 0, 0)),
            pl.BlockSpec((tf, D), lambda b, i, j: (jnp.where(j > 0, j - 1, nf - 1), 0)),
            pl.BlockSpec((1, D), lambda b, i, j: (0, 0)),
        ],
        out_specs=pl.BlockSpec((1, tm, D), lambda b, i, j: (b, i, 0)),
        out_shape=jax.ShapeDtypeStruct((B, S, D), F32),
        scratch_shapes=[pltpu.VMEM((tm, D), BF16), pltpu.VMEM((2, tm, tf), BF16)],
        compiler_params=_params(("arbitrary", "arbitrary", "arbitrary"), FFN_VMEM_LIMIT),
        name="ffn",
    )(x, mod, g_pre, w_up, w_up, w_down, g_fin)


def _proj_kernel(x_ref, mod_ref, g_ref, win_ref, gcq_ref, gckv_ref, wuq_ref, wukv_ref, gq_ref, gk_ref,
                 gdq_ref, gdk_ref, bd_ref, cos_ref, slo_ref, shi_ref,
                 q_ref, k_ref, v_ref, dq_ref, dk_ref, dv_ref, glu_ref):
    sh = mod_ref[0, 3:4, :]
    sc = mod_ref[0, 4:5, :]
    h = _adaln(x_ref[0], g_ref[...], sc, sh).astype(BF16)
    cos = cos_ref[...]
    slo = slo_ref[...]
    shi = shi_ref[...]

    pa = _dot(h, win_ref[:, SEC_A[0]:SEC_A[1]])
    cq = pa[:, 0:MLA_Q_RANK]
    ckv = pa[:, MLA_Q_RANK:MLA_Q_RANK + MLA_KV_RANK]
    kr = pa[:, MLA_Q_RANK + MLA_KV_RANK:SEC_A[1]]
    cqn = (_rms(cq, MLA_Q_RANK) * gcq_ref[...]).astype(BF16)
    ckvn = (_rms(ckv, MLA_KV_RANK) * gckv_ref[...]).astype(BF16)
    qraw = _dot(cqn, wuq_ref[...])
    kv = _dot(ckvn, wukv_ref[...])
    kr_ss = jnp.sum(kr * kr, axis=-1, keepdims=True)
    gq = gq_ref[...]
    gk = gk_ref[...]
    for hd in range(MLA_HEADS):
        c0 = hd * MLA_QK_PAD
        qh = qraw[:, c0:c0 + MLA_QK_PAD]
        qn = _rms(qh, MLA_QK) * gq
        q_ref[0, hd, :, 0:LANE] = qn[:, 0:LANE].astype(BF16)
        q_ref[0, hd, :, LANE:2 * LANE] = _rope(qn[:, LANE:2 * LANE], cos, slo, shi).astype(BF16)
        kn = kv[:, c0:c0 + LANE]
        inv = lax.rsqrt((jnp.sum(kn * kn, axis=-1, keepdims=True) + kr_ss) * (1.0 / MLA_QK) + EPS)
        k_ref[0, hd, :, 0:LANE] = (kn * inv * gk[:, 0:LANE]).astype(BF16)
        k_ref[0, hd, :, LANE:2 * LANE] = _rope(kr * inv * gk[:, LANE:2 * LANE], cos, slo, shi).astype(BF16)
        v_ref[0, hd] = kv[:, c0 + LANE:c0 + 2 * LANE].T.astype(BF16)

    bd = bd_ref[...]
    for sec, g_t, o_ref in ((SEC_DQ, gdq_ref, dq_ref), (SEC_DK, gdk_ref, dk_ref)):
        z = _dot(h, win_ref[:, sec[0]:sec[1]])
        g_all = g_t[...]
        for c0 in range(0, z.shape[1], GROUP_TILE):
            zc = z[:, c0:c0 + GROUP_TILE]
            zz = zc * zc
            zz_hi = zz.astype(BF16)
            zz_lo = (zz - zz_hi.astype(F32)).astype(BF16)
            ss = _dot(zz_hi, bd) + _dot(zz_lo, bd)
            zn = zc * lax.rsqrt(ss * (1.0 / DIFF_DK) + EPS) * g_all[:, c0:c0 + GROUP_TILE]
            for t in range(GROUP_TILE // LANE):
                o_ref[0, c0 // LANE + t] = _rope(zn[:, t * LANE:(t + 1) * LANE], cos, slo, shi).astype(BF16)
    z = _dot(h, win_ref[:, SEC_DV[0]:SEC_DV[1]])
    for hd in range(DIFF_HEADS):
        dv_ref[0, hd] = z[:, hd * LANE:(hd + 1) * LANE].T.astype(BF16)

    cu = _dot(h, win_ref[:, SEC_CU[0]:SEC_CU[1]])
    glu_ref[0] = cu[:, 0:CONV_CH] * jax.nn.sigmoid(cu[:, CONV_CH:2 * CONV_CH])


def _proj_call(x, mod, g_pre, lw, rope_tabs, *, tm):
    B, S, D = x.shape
    H = MLA_HEADS
    const = lambda shape: pl.BlockSpec(shape, lambda b, i: (0,) * len(shape))
    tab = pl.BlockSpec((tm, LANE), lambda b, i: (i, 0))
    head_out = lambda w: pl.BlockSpec((1, H, tm, w), lambda b, i: (b, 0, i, 0))
    head_shape = lambda w: jax.ShapeDtypeStruct((B, H, S, w), BF16)
    val_out = pl.BlockSpec((1, H, LANE, tm), lambda b, i: (b, 0, 0, i))
    val_shape = jax.ShapeDtypeStruct((B, H, LANE, S), BF16)
    return pl.pallas_call(
        _proj_kernel,
        grid=(B, S // tm),
        in_specs=[
            pl.BlockSpec((1, tm, D), lambda b, i: (b, i, 0)),
            pl.BlockSpec((1, N_MOD, D), lambda b, i: (b, 0, 0)),
            const((1, D)),
            const((D, IN_COLS_PAD)),
            const((1, MLA_Q_RANK)),
            const((1, MLA_KV_RANK)),
            const((MLA_Q_RANK, H * MLA_QK_PAD)),
            const((MLA_KV_RANK, H * MLA_QK_PAD)),
            const((1, MLA_QK_PAD)),
            const((1, MLA_QK_PAD)),
            const((1, DIFF_HEADS * LANE)),
            const((1, DIFF_HEADS * LANE)),
            const((GROUP_TILE, GROUP_TILE)),
            tab, tab, tab,
        ],
        out_specs=[head_out(MLA_QK_PAD), head_out(MLA_QK_PAD), val_out,
                   head_out(LANE), head_out(LANE), val_out,
                   pl.BlockSpec((1, tm, CONV_CH), lambda b, i: (b, i, 0))],
        out_shape=[head_shape(MLA_QK_PAD), head_shape(MLA_QK_PAD), val_shape,
                   head_shape(LANE), head_shape(LANE), val_shape,
                   jax.ShapeDtypeStruct((B, S, CONV_CH), F32)],
        compiler_params=_params(("arbitrary", "arbitrary")),
        name="proj",
    )(x, mod, g_pre, lw["w_in"], lw["g_cq"], lw["g_ckv"], lw["w_uq"], lw["w_ukv"], lw["g_q"], lw["g_k"],
      lw["g_dq"], lw["g_dk"], lw["bd"], *rope_tabs)


def _softmax_step(s, vt, m_ref, l_ref, acc_ref):
    m_prev = m_ref[...]
    m_new = jnp.maximum(m_prev, jnp.max(s, axis=-1, keepdims=True))
    alpha = jnp.exp2(m_prev - m_new)
    p = jnp.exp2(s - m_new)
    l_ref[...] = alpha * l_ref[...] + jnp.sum(p, axis=-1, keepdims=True)
    acc_ref[...] = alpha * acc_ref[...] + _dot_nt(p.astype(BF16), vt)
    m_ref[...] = m_new


def _split_maps(q):
    lane = lax.broadcasted_iota(jnp.int32, q.shape, 1)
    zero = jnp.zeros_like(q)
    return jnp.where(lane < DIFF_DK, q, zero), jnp.where(lane >= DIFF_DK, q, zero)


def _lambda(lam_ref, lam_init):
    lv = lam_ref[...]
    return (jnp.exp(jnp.sum(lv[0:1] * lv[1:2], axis=-1, keepdims=True))
            - jnp.exp(jnp.sum(lv[2:3] * lv[3:4], axis=-1, keepdims=True)) + lam_init)


def _mla_attn_kernel(q_ref, k_ref, v_ref, o_ref, m_ref, l_ref, acc_ref, *, tk):
    S = k_ref.shape[2]
    q = q_ref[0, 0]
    m_ref[...] = jnp.full(m_ref.shape, -jnp.inf, F32)
    l_ref[...] = jnp.zeros(l_ref.shape, F32)
    acc_ref[...] = jnp.zeros(acc_ref.shape, F32)

    def body(c, carry):
        r0 = pl.multiple_of(c * tk, tk)
        s = _dot_nt(q, k_ref[0, 0, pl.ds(r0, tk), :])
        _softmax_step(s, v_ref[0, 0, :, pl.ds(r0, tk)], m_ref, l_ref, acc_ref)
        return carry

    lax.fori_loop(0, S // tk, body, 0)
    o_ref[0] = (acc_ref[...] / l_ref[...]).astype(o_ref.dtype)


def _diff_attn_kernel(q_ref, k_ref, v_ref, lam_ref, gsub_ref, o_ref,
                      m1_ref, l1_ref, a1_ref, m2_ref, l2_ref, a2_ref, *, tk, lam_init):
    S = k_ref.shape[2]
    q1, q2 = _split_maps(q_ref[0, 0])
    for m_ref, l_ref, a_ref in ((m1_ref, l1_ref, a1_ref), (m2_ref, l2_ref, a2_ref)):
        m_ref[...] = jnp.full(m_ref.shape, -jnp.inf, F32)
        l_ref[...] = jnp.zeros(l_ref.shape, F32)
        a_ref[...] = jnp.zeros(a_ref.shape, F32)

    def body(c, carry):
        r0 = pl.multiple_of(c * tk, tk)
        ks = k_ref[0, 0, pl.ds(r0, tk), :]
        vt = v_ref[0, 0, :, pl.ds(r0, tk)]
        _softmax_step(_dot_nt(q1, ks), vt, m1_ref, l1_ref, a1_ref)
        _softmax_step(_dot_nt(q2, ks), vt, m2_ref, l2_ref, a2_ref)
        return carry

    lax.fori_loop(0, S // tk, body, 0)
    o = a1_ref[...] / l1_ref[...] - _lambda(lam_ref, lam_init) * (a2_ref[...] / l2_ref[...])
    o = _rms(o, DIFF_DV) * gsub_ref[...] * (1.0 - lam_init)
    o_ref[0] = o.astype(o_ref.dtype)


def _shifted_step(q, ks, vt, bound, ls_ref, acc_ref):
    pt = jnp.exp2(_dot_nt(ks, q) - bound)
    part = pt[0:8, :]
    for t in range(1, pt.shape[0] // 8):
        part = part + pt[8 * t:8 * t + 8, :]
    ls_ref[...] += part
    acc_ref[...] += _dot(vt, pt.astype(BF16))


def _mla_shifted_kernel(b_ref, q_ref, k_ref, v_ref, o_ref, ls_ref, acc_ref, *, tk):
    S = k_ref.shape[2]
    q = q_ref[0, 0]
    bound = b_ref[0]
    ls_ref[...] = jnp.zeros(ls_ref.shape, F32)
    acc_ref[...] = jnp.zeros(acc_ref.shape, F32)

    def body(c, carry):
        r0 = pl.multiple_of(c * tk, tk)
        _shifted_step(q, k_ref[0, 0, pl.ds(r0, tk), :], v_ref[0, 0, :, pl.ds(r0, tk)], bound, ls_ref, acc_ref)
        return carry

    lax.fori_loop(0, S // tk, body, 0, unroll=2)
    ot = acc_ref[...] / jnp.sum(ls_ref[...], axis=0, keepdims=True)
    o_ref[0] = ot.T.astype(o_ref.dtype)


def _diff_shifted_kernel(b_ref, q_ref, k_ref, v_ref, lam_ref, gsub_ref, o_ref,
                         ls1_ref, a1_ref, ls2_ref, a2_ref, *, tk, lam_init):
    S = k_ref.shape[2]
    q1, q2 = _split_maps(q_ref[0, 0])
    bound = b_ref[0]
    for r in (ls1_ref, a1_ref, ls2_ref, a2_ref):
        r[...] = jnp.zeros(r.shape, F32)

    def body(c, carry):
        r0 = pl.multiple_of(c * tk, tk)
        ks = k_ref[0, 0, pl.ds(r0, tk), :]
        vt = v_ref[0, 0, :, pl.ds(r0, tk)]
        _shifted_step(q1, ks, vt, bound, ls1_ref, a1_ref)
        _shifted_step(q2, ks, vt, bound, ls2_ref, a2_ref)
        return carry

    lax.fori_loop(0, S // tk, body, 0, unroll=2)
    o1 = a1_ref[...] / jnp.sum(ls1_ref[...], axis=0, keepdims=True)
    o2 = a2_ref[...] / jnp.sum(ls2_ref[...], axis=0, keepdims=True)
    o = (o1 - _lambda(lam_ref, lam_init) * o2).T
    o = _rms(o, DIFF_DV) * gsub_ref[...] * (1.0 - lam_init)
    o_ref[0] = o.astype(o_ref.dtype)


def _attn_call(q, k, v, extra, bound, *, tq, tk, lam_init=None):
    B, H, S, dqk = q.shape
    dv = v.shape[2]
    tk_safe = min(tk, 512)
    stat = pltpu.VMEM((tq, 1), F32)
    acc = pltpu.VMEM((tq, dv), F32)
    part_t = pltpu.VMEM((8, tq), F32)
    acc_t = pltpu.VMEM((dv, tq), F32)
    qkv_specs = [
        pl.BlockSpec((1, 1, tq, dqk), lambda b, h, i: (b, h, i, 0)),
        pl.BlockSpec((1, 1, S, dqk), lambda b, h, i: (b, h, 0, 0)),
        pl.BlockSpec((1, 1, dv, S), lambda b, h, i: (b, h, 0, 0)),
    ]
    extra_specs = [pl.BlockSpec(e.shape, lambda b, h, i: (0, 0)) for e in extra]
    smem = pl.BlockSpec(memory_space=pltpu.SMEM)
    if lam_init is None:
        safe_kern = functools.partial(_mla_attn_kernel, tk=tk_safe)
        fast_kern = functools.partial(_mla_shifted_kernel, tk=tk)
        safe_scratch, fast_scratch = [stat, stat, acc], [part_t, acc_t]
        name = "mla_attn"
    else:
        safe_kern = functools.partial(_diff_attn_kernel, tk=tk_safe, lam_init=lam_init)
        fast_kern = functools.partial(_diff_shifted_kernel, tk=tk, lam_init=lam_init)
        safe_scratch, fast_scratch = [stat, stat, acc, stat, stat, acc], [part_t, acc_t, part_t, acc_t]
        name = "diff_attn"
    common = dict(
        grid=(B, H, S // tq),
        out_specs=pl.BlockSpec((1, tq, dv), lambda b, h, i: (b, i, h)),
        out_shape=jax.ShapeDtypeStruct((B, S, H * dv), BF16),
        compiler_params=_params(("arbitrary", "arbitrary", "arbitrary")),
    )

    def fast(bound, q, k, v, *extra):
        return pl.pallas_call(fast_kern, in_specs=[smem] + qkv_specs + extra_specs, scratch_shapes=fast_scratch,
                              name=name + "_shifted", **common)((bound * LOG2E).reshape(1), q, k, v, *extra)

    def safe(bound, q, k, v, *extra):
        return pl.pallas_call(safe_kern, in_specs=qkv_specs + extra_specs, scratch_shapes=safe_scratch,
                              name=name, **common)(q, k, v, *extra)

    return lax.cond(bound <= MAX_SHIFT_BOUND, fast, safe, bound, q, k, v, *extra)


def _conv_kernel(prev_ref, cur_ref, next_ref, w_ref, b_ref, g_ref, beta_ref, o_ref, ext_ref, *, rows):
    i = pl.program_id(1)
    ni = pl.num_programs(1)
    tm = cur_ref.shape[1]
    halo = CONV_HALO
    prev = prev_ref[0]
    nxt = next_ref[0]
    ext_ref[0, 0:halo, :] = jnp.where(i > 0, prev, jnp.zeros_like(prev))
    ext_ref[0, halo:halo + tm, :] = cur_ref[0]
    ext_ref[0, halo + tm:2 * halo + tm, :] = jnp.where(i < ni - 1, nxt, jnp.zeros_like(nxt))
    off = halo - CONV_WIDTH // 2
    span = tm + 8 * ((off + CONV_WIDTH - 1) // 8)
    for s in range(1, 8):
        for c0 in range(0, span, rows):
            n = min(rows, span - c0)
            ext_ref[s, c0:c0 + n, :] = ext_ref[0, c0 + s:c0 + s + n, :]
    w = w_ref[...]
    bias = b_ref[...]
    g = g_ref[...]
    beta = beta_ref[...]

    def body(r, carry):
        r0 = pl.multiple_of(r * rows, rows)
        acc = jnp.zeros((rows, CONV_CH), F32)
        for j in range(CONV_WIDTH):
            a, s = divmod(j + off, 8)
            acc = acc + ext_ref[s, pl.ds(r0 + 8 * a, rows), :] * w[j:j + 1, :]
        acc = acc + bias
        mu = jnp.mean(acc, axis=-1, keepdims=True)
        d = acc - mu
        var = jnp.mean(d * d, axis=-1, keepdims=True)
        y = d * lax.rsqrt(var + EPS) * g + beta
        o_ref[0, pl.ds(r0, rows), :] = (y * jax.nn.sigmoid(y)).astype(o_ref.dtype)
        return carry

    lax.fori_loop(0, tm // rows, body, 0)


def _conv_call(glu, w, b, g, beta, *, tm, rows=64):
    B, S, C = glu.shape
    hb = tm // CONV_HALO
    nh = S // CONV_HALO
    const = lambda shape: pl.BlockSpec(shape, lambda b_, i: (0, 0))
    return pl.pallas_call(
        functools.partial(_conv_kernel, rows=min(rows, tm)),
        grid=(B, S // tm),
        in_specs=[
            pl.BlockSpec((1, CONV_HALO, C), lambda b_, i: (b_, jnp.maximum(i * hb - 1, 0), 0)),
            pl.BlockSpec((1, tm, C), lambda b_, i: (b_, i, 0)),
            pl.BlockSpec((1, CONV_HALO, C), lambda b_, i: (b_, jnp.minimum((i + 1) * hb, nh - 1), 0)),
            const(w.shape), const((1, C)), const((1, C)), const((1, C)),
        ],
        out_specs=pl.BlockSpec((1, tm, C), lambda b_, i: (b_, i, 0)),
        out_shape=jax.ShapeDtypeStruct((B, S, C), BF16),
        scratch_shapes=[pltpu.VMEM((8, tm + 2 * CONV_HALO, C), F32)],
        compiler_params=_params(("arbitrary", "arbitrary")),
        name="conv",
    )(glu, glu, glu, w, b, g, beta)


def _out_kernel(x_ref, mod_ref, mla_ref, dif_ref, cnv_ref, w_ref, o_ref):
    n1 = mla_ref.shape[-1]
    n2 = n1 + dif_ref.shape[-1]
    acc = (_dot(mla_ref[0], w_ref[0:n1, :]) + _dot(dif_ref[0], w_ref[n1:n2, :])
           + _dot(cnv_ref[0], w_ref[n2:, :]))
    o_ref[0] = x_ref[0] + mod_ref[0, 5:6, :] * acc


def _out_call(x, mod, o_mla, o_dif, o_cnv, w_out, *, tm):
    B, S, D = x.shape
    tok = lambda w: pl.BlockSpec((1, tm, w), lambda b, i: (b, i, 0))
    return pl.pallas_call(
        _out_kernel,
        grid=(B, S // tm),
        in_specs=[tok(D), pl.BlockSpec((1, N_MOD, D), lambda b, i: (b, 0, 0)),
                  tok(o_mla.shape[-1]), tok(o_dif.shape[-1]), tok(o_cnv.shape[-1]),
                  pl.BlockSpec(w_out.shape, lambda b, i: (0, 0))],
        out_specs=tok(D),
        out_shape=jax.ShapeDtypeStruct((B, S, D), F32),
        compiler_params=_params(("arbitrary", "arbitrary")),
        name="out",
    )(x, mod, o_mla, o_dif, o_cnv, w_out)


def _rope_tables(S):
    half = DIFF_DK // 2
    inv_freq = jnp.float32(ROPE_THETA) ** (-jnp.arange(half, dtype=F32) * (2.0 / DIFF_DK))
    ang = jnp.arange(S, dtype=jnp.int32).astype(F32)[:, None] * inv_freq[None, :]
    c, s, z = jnp.cos(ang), jnp.sin(ang), jnp.zeros_like(ang)
    return (jnp.concatenate([c, c, c, c], axis=1),
            jnp.concatenate([-s, z, -s, z], axis=1),
            jnp.concatenate([z, s, z, s], axis=1))


def _layer_weights(l, w_in, g_cq, g_ckv, w_uq, w_ukv, g_mla_q, g_mla_k, g_diff_q, g_diff_k,
                   lam_q1, lam_k1, lam_q2, lam_k2, g_diff_sub, w_dw, b_dw, g_conv, b_conv, w_out):
    wi = w_in[l]
    o_rope = MLA_Q_RANK + MLA_KV_RANK
    o_dq = o_rope + MLA_ROPE
    w_in_p = jnp.concatenate(
        [wi[:, :o_dq], jnp.zeros((D_MODEL, LANE - MLA_ROPE), wi.dtype), wi[:, o_dq:]], axis=1).astype(BF16)
    w_uq_p = jnp.pad(w_uq[l].reshape(MLA_Q_RANK, MLA_HEADS, MLA_QK),
                     ((0, 0), (0, 0), (0, MLA_QK_PAD - MLA_QK))).reshape(MLA_Q_RANK, -1).astype(BF16)
    pad_qk = lambda g: jnp.pad(g, (0, MLA_QK_PAD - MLA_QK)).reshape(1, MLA_QK_PAD)
    return dict(
        w_in=w_in_p, g_cq=g_cq[l].reshape(1, -1), g_ckv=g_ckv[l].reshape(1, -1),
        w_uq=w_uq_p, w_ukv=w_ukv[l].astype(BF16),
        g_q=pad_qk(g_mla_q[l] * (LOG2E / math.sqrt(MLA_QK))), g_k=pad_qk(g_mla_k[l]),
        g_dq=jnp.tile(g_diff_q[l] * (LOG2E / math.sqrt(DIFF_DK)), 2 * DIFF_HEADS).reshape(1, -1),
        g_dk=jnp.tile(g_diff_k[l], 2 * DIFF_HEADS).reshape(1, -1),
        bd=jnp.asarray(np.kron(np.eye(GROUP_TILE // DIFF_DK), np.ones((DIFF_DK, DIFF_DK))), BF16),
        lam=jnp.stack([lam_q1[l], lam_k1[l], lam_q2[l], lam_k2[l]]),
        b_mla=1.01 * math.sqrt(MLA_QK) * jnp.max(jnp.abs(g_mla_q[l])) * jnp.max(jnp.abs(g_mla_k[l])),
        b_dif=1.01 * math.sqrt(DIFF_DK) * jnp.max(jnp.abs(g_diff_q[l])) * jnp.max(jnp.abs(g_diff_k[l])),
        g_sub=g_diff_sub[l].reshape(1, -1),
        w_dw=jnp.pad(w_dw[l], ((0, 1), (0, 0))), b_dw=b_dw[l].reshape(1, -1),
        g_conv=g_conv[l].reshape(1, -1), b_conv=b_conv[l].reshape(1, -1),
        w_out=w_out[l].astype(BF16),
    )


def _tile(S, t):
    return min(S, t)


def kernel(x_prompt, x_sample, c_prompt, c_sample, w_ada, b_ada, g_norm, w_ffn1_in, w_ffn1_out, w_ffn2_in, w_ffn2_out, w_in, g_cq, g_ckv, w_uq, w_ukv, g_mla_q, g_mla_k, g_diff_q, g_diff_k, lam_q1, lam_k1, lam_q2, lam_k2, g_diff_sub, w_dw, b_dw, g_conv, b_conv, w_out):
    L = w_ada.shape[0]
    bp, bs = c_prompt.shape[0], c_sample.shape[0]
    rows = -(-(bp + bs) // 8) * 8
    c_all = jnp.pad(jnp.concatenate([c_prompt, c_sample], axis=0), ((0, rows - bp - bs), (0, 0)))
    mod_all = _ada_call(c_all, w_ada, b_ada)

    def chunked_up(w):
        return w.astype(BF16).reshape(L, D_MODEL, 2 * D_FF // FFN_TF, FFN_TF).transpose(0, 2, 1, 3)

    up1, dn1 = chunked_up(w_ffn1_in), w_ffn1_out.astype(BF16)
    up2, dn2 = chunked_up(w_ffn2_in), w_ffn2_out.astype(BF16)
    layers = [_layer_weights(l, w_in, g_cq, g_ckv, w_uq, w_ukv, g_mla_q, g_mla_k, g_diff_q, g_diff_k,
                             lam_q1, lam_k1, lam_q2, lam_k2, g_diff_sub, w_dw, b_dw, g_conv, b_conv, w_out)
              for l in range(L)]

    def trunk(x, row0):
        B, S, D = x.shape
        tm, tm_ffn, tq, tk = (_tile(S, t) for t in (TOKEN_TILE, FFN_TOKEN_TILE, ATTN_Q_TILE, ATTN_K_TILE))
        tabs = _rope_tables(S)
        for l in range(L):
            lw = layers[l]
            mod = mod_all[l, row0:row0 + B].reshape(B, N_MOD, D)
            gn = g_norm[l]
            x = _ffn_call(x, mod, gn[0:1], up1[l], dn1[l], gn[3:4], sub=0, final_norm=False, tm=tm_ffn)
            q, k, v, dq, dk, dv, glu = _proj_call(x, mod, gn[1:2], lw, tabs, tm=tm)
            o_mla = _attn_call(q, k, v, (), lw["b_mla"], tq=tq, tk=tk)
            lam_init = 0.8 - 0.6 * math.exp(-0.3 * l)
            o_dif = _attn_call(dq, dk, dv, (lw["lam"], lw["g_sub"]), lw["b_dif"], tq=tq, tk=tk, lam_init=lam_init)
            o_cnv = _conv_call(glu, lw["w_dw"], lw["b_dw"], lw["g_conv"], lw["b_conv"], tm=tm)
            x = _out_call(x, mod, o_mla, o_dif, o_cnv, lw["w_out"], tm=tm)
            x = _ffn_call(x, mod, gn[2:3], up2[l], dn2[l], gn[3:4], sub=2, final_norm=True, tm=tm_ffn)
        return x

    return (trunk(x_prompt, 0), trunk(x_sample, bp))
```

```python
import functools
import math

import numpy as np
import jax
import jax.numpy as jnp
from jax import lax
from jax.experimental import pallas as pl
from jax.experimental.pallas import tpu as pltpu

F32 = jnp.float32
BF16 = jnp.bfloat16

D_MODEL = 2048
MLA_HEADS = 6
MLA_Q_RANK = 512
MLA_KV_RANK = 256
MLA_NOPE = 128
MLA_ROPE = 64
MLA_V = 128
MLA_QK = MLA_NOPE + MLA_ROPE
MLA_QK_PAD = 256
DIFF_HEADS = 6
DIFF_DK = 64
DIFF_DV = 128
CONV_CH = 512
CONV_WIDTH = 31
CONV_HALO = 16
D_FF = 5632
N_MOD = 9
ROPE_THETA = 10000.0
EPS = 1e-6
LANE = 128

SEC_A = (0, 896)
SEC_DQ = (896, 1664)
SEC_DK = (1664, 2432)
SEC_DV = (2432, 3200)
SEC_CU = (3200, 4224)
IN_COLS_PAD = 4224

VMEM_LIMIT = 56 * 1024 * 1024
FFN_VMEM_LIMIT = 60 * 1024 * 1024
MAX_SHIFT_BOUND = 40.0
LOG2E = 1.4426950408889634
TOKEN_TILE = 512
FFN_TOKEN_TILE = 1024
ATTN_Q_TILE = 1024
ATTN_K_TILE = 2048
FFN_TF = 512
GROUP_TILE = 256


def _dot(a, b):
    return jnp.dot(a, b, preferred_element_type=F32)


def _dot_nt(a, b):
    return lax.dot_general(a, b, (((1,), (1,)), ((), ())), preferred_element_type=F32)


def _rms(x, width):
    return x * lax.rsqrt(jnp.sum(x * x, axis=-1, keepdims=True) * (1.0 / width) + EPS)


def _adaln(x, g, sc, sh):
    return (_rms(x, x.shape[-1]) * g) * (1.0 + sc) + sh


def _rope(v, cos, sin_lo, sin_hi):
    return v * cos + pltpu.roll(v, 96, 1) * sin_lo + pltpu.roll(v, 32, 1) * sin_hi


def _params(sem, vmem=VMEM_LIMIT):
    return pltpu.CompilerParams(dimension_semantics=sem, vmem_limit_bytes=vmem)


def _ada_kernel(c_ref, w_ref, b_ref, o_ref):
    c = c_ref[...]
    s = c * jax.nn.sigmoid(c)
    s_hi = s.astype(BF16)
    s_lo = (s - s_hi.astype(F32)).astype(BF16)
    w = w_ref[0]
    w_hi = w.astype(BF16)
    w_lo = (w - w_hi.astype(F32)).astype(BF16)
    o_ref[0] = _dot(s_hi, w_hi) + _dot(s_lo, w_hi) + _dot(s_hi, w_lo) + b_ref[0]


def _ada_call(c_all, w_ada, b_ada, tn=512):
    L, D, N = w_ada.shape
    R = c_all.shape[0]
    return pl.pallas_call(
        _ada_kernel,
        grid=(L, N // tn),
        in_specs=[
            pl.BlockSpec((R, D), lambda l, n: (0, 0)),
            pl.BlockSpec((1, D, tn), lambda l, n: (l, 0, n)),
            pl.BlockSpec((1, 1, tn), lambda l, n: (l, 0, n)),
        ],
        out_specs=pl.BlockSpec((1, R, tn), lambda l, n: (l, 0, n)),
        out_shape=jax.ShapeDtypeStruct((L, R, N), F32),
        compiler_params=_params(("arbitrary", "arbitrary")),
        name="ada",
    )(c_all, w_ada, b_ada.reshape(L, 1, N))


def _ffn_kernel(x_ref, mod_ref, g_ref, wa_ref, wb_ref, wd_ref, gf_ref, o_ref, h_ref, act_ref, *, sub, final_norm):
    j = pl.program_id(2)
    nf = pl.num_programs(2) - 1

    def gated(h):
        a = _dot(h, wa_ref[...])
        b = _dot(h, wb_ref[...])
        return (a * jax.nn.sigmoid(a) * b).astype(BF16)

    def up(slot):
        act_ref[slot] = gated(h_ref[...])

    def down(slot):
        return _dot(act_ref[slot], wd_ref[...])

    @pl.when(j == 0)
    def _():
        sh = mod_ref[0, 3 * sub:3 * sub + 1, :]
        sc = mod_ref[0, 3 * sub + 1:3 * sub + 2, :]
        h_ref[...] = _adaln(x_ref[0], g_ref[...], sc, sh).astype(BF16)
        o_ref[0] = jnp.zeros(o_ref.shape[1:], F32)
        up(0)

    @pl.when((j > 0) & (j < nf))
    def _():
        up(j % 2)
        o_ref[0] += down((j - 1) % 2)

    @pl.when(j == nf)
    def _():
        ga = mod_ref[0, 3 * sub + 2:3 * sub + 3, :]
        y = x_ref[0] + 0.5 * ga * (o_ref[0] + down((nf - 1) % 2))
        if final_norm:
            y = _rms(y, y.shape[-1]) * gf_ref[...]
        o_ref[0] = y


def _ffn_call(x, mod, g_pre, w_up, w_down, g_fin, *, layer, sub, final_norm, tm):
    B, S, D = x.shape
    tf = FFN_TF
    nf = D_FF // tf
    kern = functools.partial(_ffn_kernel, sub=sub, final_norm=final_norm)
    return pl.pallas_call(
        kern,
        grid=(B, S // tm, nf + 1),
        in_specs=[
            pl.BlockSpec((1, tm, D), lambda b, i, j: (b, i, 0)),
            pl.BlockSpec((1, N_MOD, D), lambda b, i, j: (b, 0, 0)),
            pl.BlockSpec((1, D), lambda b, i, j: (0, 0)),
            pl.BlockSpec((None, None, D, tf), lambda b, i, j: (layer, jnp.where(j < nf, j, 0), 0, 0)),
            pl.BlockSpec((None, None, D, tf), lambda b, i, j: (layer, jnp.where(j < nf, j, 0) + nf, 0, 0)),
            pl.BlockSpec((None, tf, D), lambda b, i, j: (layer, jnp.where(j > 0, j - 1, nf - 1), 0)),
            pl.BlockSpec((1, D), lambda b, i, j: (0, 0)),
        ],
        out_specs=pl.BlockSpec((1, tm, D), lambda b, i, j: (b, i, 0)),
        out_shape=jax.ShapeDtypeStruct((B, S, D), F32),
        scratch_shapes=[pltpu.VMEM((tm, D), BF16), pltpu.VMEM((2, tm, tf), BF16)],
        compiler_params=_params(("arbitrary", "arbitrary", "arbitrary"), FFN_VMEM_LIMIT),
        name="ffn",
    )(x, mod, g_pre, w_up, w_up, w_down, g_fin)


def _proj_kernel(x_ref, mod_ref, g_ref, win_ref, gcq_ref, gckv_ref, wuq_ref, wukv_ref, gq_ref, gk_ref,
                 gdq_ref, gdk_ref, bd_ref, cos_ref, slo_ref, shi_ref,
                 q_ref, k_ref, v_ref, dq_ref, dk_ref, dv_ref, glu_ref):
    sh = mod_ref[0, 3:4, :]
    sc = mod_ref[0, 4:5, :]
    h = _adaln(x_ref[0], g_ref[...], sc, sh).astype(BF16)
    cos = cos_ref[...]
    slo = slo_ref[...]
    shi = shi_ref[...]

    pa = _dot(h, win_ref[:, SEC_A[0]:SEC_A[1]])
    cq = pa[:, 0:MLA_Q_RANK]
    ckv = pa[:, MLA_Q_RANK:MLA_Q_RANK + MLA_KV_RANK]
    kr = pa[:, MLA_Q_RANK + MLA_KV_RANK:SEC_A[1]]
    cqn = (_rms(cq, MLA_Q_RANK) * gcq_ref[...]).astype(BF16)
    ckvn = (_rms(ckv, MLA_KV_RANK) * gckv_ref[...]).astype(BF16)
    qraw = _dot(cqn, wuq_ref[...])
    kv = _dot(ckvn, wukv_ref[...])
    kr_ss = jnp.sum(kr * kr, axis=-1, keepdims=True)
    gq = gq_ref[...]
    gk = gk_ref[...]
    for hd in range(MLA_HEADS):
        c0 = hd * MLA_QK_PAD
        qh = qraw[:, c0:c0 + MLA_QK_PAD]
        qn = _rms(qh, MLA_QK) * gq
        q_ref[0, hd, :, 0:LANE] = qn[:, 0:LANE].astype(BF16)
        q_ref[0, hd, :, LANE:2 * LANE] = _rope(qn[:, LANE:2 * LANE], cos, slo, shi).astype(BF16)
        kn = kv[:, c0:c0 + LANE]
        inv = lax.rsqrt((jnp.sum(kn * kn, axis=-1, keepdims=True) + kr_ss) * (1.0 / MLA_QK) + EPS)
        k_ref[0, hd, :, 0:LANE] = (kn * inv * gk[:, 0:LANE]).astype(BF16)
        k_ref[0, hd, :, LANE:2 * LANE] = _rope(kr * inv * gk[:, LANE:2 * LANE], cos, slo, shi).astype(BF16)
        v_ref[0, hd] = kv[:, c0 + LANE:c0 + 2 * LANE].T.astype(BF16)

    bd = bd_ref[...]
    for sec, g_t, o_ref in ((SEC_DQ, gdq_ref, dq_ref), (SEC_DK, gdk_ref, dk_ref)):
        z = _dot(h, win_ref[:, sec[0]:sec[1]])
        g_all = g_t[...]
        for c0 in range(0, z.shape[1], GROUP_TILE):
            zc = z[:, c0:c0 + GROUP_TILE]
            zz = zc * zc
            zz_hi = zz.astype(BF16)
            zz_lo = (zz - zz_hi.astype(F32)).astype(BF16)
            ss = _dot(zz_hi, bd) + _dot(zz_lo, bd)
            zn = zc * lax.rsqrt(ss * (1.0 / DIFF_DK) + EPS) * g_all[:, c0:c0 + GROUP_TILE]
            for t in range(GROUP_TILE // LANE):
                o_ref[0, c0 // LANE + t] = _rope(zn[:, t * LANE:(t + 1) * LANE], cos, slo, shi).astype(BF16)
    z = _dot(h, win_ref[:, SEC_DV[0]:SEC_DV[1]])
    for hd in range(DIFF_HEADS):
        dv_ref[0, hd] = z[:, hd * LANE:(hd + 1) * LANE].T.astype(BF16)

    cu = _dot(h, win_ref[:, SEC_CU[0]:SEC_CU[1]])
    glu_ref[0] = cu[:, 0:CONV_CH] * jax.nn.sigmoid(cu[:, CONV_CH:2 * CONV_CH])


def _proj_call(x, mod, g_pre, lw, rope_tabs, *, tm):
    B, S, D = x.shape
    H = MLA_HEADS
    const = lambda shape: pl.BlockSpec(shape, lambda b, i: (0,) * len(shape))
    tab = pl.BlockSpec((tm, LANE), lambda b, i: (i, 0))
    head_out = lambda w: pl.BlockSpec((1, H, tm, w), lambda b, i: (b, 0, i, 0))
    head_shape = lambda w: jax.ShapeDtypeStruct((B, H, S, w), BF16)
    val_out = pl.BlockSpec((1, H, LANE, tm), lambda b, i: (b, 0, 0, i))
    val_shape = jax.ShapeDtypeStruct((B, H, LANE, S), BF16)
    return pl.pallas_call(
        _proj_kernel,
        grid=(B, S // tm),
        in_specs=[
            pl.BlockSpec((1, tm, D), lambda b, i: (b, i, 0)),
            pl.BlockSpec((1, N_MOD, D), lambda b, i: (b, 0, 0)),
            const((1, D)),
            const((D, IN_COLS_PAD)),
            const((1, MLA_Q_RANK)),
            const((1, MLA_KV_RANK)),
            const((MLA_Q_RANK, H * MLA_QK_PAD)),
            const((MLA_KV_RANK, H * MLA_QK_PAD)),
            const((1, MLA_QK_PAD)),
            const((1, MLA_QK_PAD)),
            const((1, DIFF_HEADS * LANE)),
            const((1, DIFF_HEADS * LANE)),
            const((GROUP_TILE, GROUP_TILE)),
            tab, tab, tab,
        ],
        out_specs=[head_out(MLA_QK_PAD), head_out(MLA_QK_PAD), val_out,
                   head_out(LANE), head_out(LANE), val_out,
                   pl.BlockSpec((1, tm, CONV_CH), lambda b, i: (b, i, 0))],
        out_shape=[head_shape(MLA_QK_PAD), head_shape(MLA_QK_PAD), val_shape,
                   head_shape(LANE), head_shape(LANE), val_shape,
                   jax.ShapeDtypeStruct((B, S, CONV_CH), F32)],
        compiler_params=_params(("arbitrary", "arbitrary")),
        name="proj",
    )(x, mod, g_pre, lw["w_in"], lw["g_cq"], lw["g_ckv"], lw["w_uq"], lw["w_ukv"], lw["g_q"], lw["g_k"],
      lw["g_dq"], lw["g_dk"], lw["bd"], *rope_tabs)


def _softmax_step(s, vt, m_ref, l_ref, acc_ref):
    m_prev = m_ref[...]
    m_new = jnp.maximum(m_prev, jnp.max(s, axis=-1, keepdims=True))
    alpha = jnp.exp2(m_prev - m_new)
    p = jnp.exp2(s - m_new)
    l_ref[...] = alpha * l_ref[...] + jnp.sum(p, axis=-1, keepdims=True)
    acc_ref[...] = alpha * acc_ref[...] + _dot_nt(p.astype(BF16), vt)
    m_ref[...] = m_new


def _split_maps(q):
    lane = lax.broadcasted_iota(jnp.int32, q.shape, 1)
    zero = jnp.zeros_like(q)
    return jnp.where(lane < DIFF_DK, q, zero), jnp.where(lane >= DIFF_DK, q, zero)


def _lambda(lam_ref, lam_init):
    lv = lam_ref[...]
    return (jnp.exp(jnp.sum(lv[0:1] * lv[1:2], axis=-1, keepdims=True))
            - jnp.exp(jnp.sum(lv[2:3] * lv[3:4], axis=-1, keepdims=True)) + lam_init)


def _mla_attn_kernel(q_ref, k_ref, v_ref, o_ref, m_ref, l_ref, acc_ref, *, tk):
    S = k_ref.shape[2]
    q = q_ref[0, 0]
    m_ref[...] = jnp.full(m_ref.shape, -jnp.inf, F32)
    l_ref[...] = jnp.zeros(l_ref.shape, F32)
    acc_ref[...] = jnp.zeros(acc_ref.shape, F32)

    def body(c, carry):
        r0 = pl.multiple_of(c * tk, tk)
        s = _dot_nt(q, k_ref[0, 0, pl.ds(r0, tk), :])
        _softmax_step(s, v_ref[0, 0, :, pl.ds(r0, tk)], m_ref, l_ref, acc_ref)
        return carry

    lax.fori_loop(0, S // tk, body, 0)
    o_ref[0] = (acc_ref[...] / l_ref[...]).astype(o_ref.dtype)


def _diff_attn_kernel(q_ref, k_ref, v_ref, lam_ref, gsub_ref, o_ref,
                      m1_ref, l1_ref, a1_ref, m2_ref, l2_ref, a2_ref, *, tk, lam_init):
    S = k_ref.shape[2]
    q1, q2 = _split_maps(q_ref[0, 0])
    for m_ref, l_ref, a_ref in ((m1_ref, l1_ref, a1_ref), (m2_ref, l2_ref, a2_ref)):
        m_ref[...] = jnp.full(m_ref.shape, -jnp.inf, F32)
        l_ref[...] = jnp.zeros(l_ref.shape, F32)
        a_ref[...] = jnp.zeros(a_ref.shape, F32)

    def body(c, carry):
        r0 = pl.multiple_of(c * tk, tk)
        ks = k_ref[0, 0, pl.ds(r0, tk), :]
        vt = v_ref[0, 0, :, pl.ds(r0, tk)]
        _softmax_step(_dot_nt(q1, ks), vt, m1_ref, l1_ref, a1_ref)
        _softmax_step(_dot_nt(q2, ks), vt, m2_ref, l2_ref, a2_ref)
        return carry

    lax.fori_loop(0, S // tk, body, 0)
    o = a1_ref[...] / l1_ref[...] - _lambda(lam_ref, lam_init) * (a2_ref[...] / l2_ref[...])
    o = _rms(o, DIFF_DV) * gsub_ref[...] * (1.0 - lam_init)
    o_ref[0] = o.astype(o_ref.dtype)


def _shifted_step(q, ks, vt, bound, ls_ref, acc_ref):
    pt = jnp.exp2(_dot_nt(ks, q) - bound)
    part = pt[0:8, :]
    for t in range(1, pt.shape[0] // 8):
        part = part + pt[8 * t:8 * t + 8, :]
    ls_ref[...] += part
    acc_ref[...] += _dot(vt, pt.astype(BF16))


def _mla_shifted_kernel(b_ref, q_ref, k_ref, v_ref, o_ref, ls_ref, acc_ref, *, tk):
    S = k_ref.shape[2]
    q = q_ref[0, 0]
    bound = b_ref[0]
    ls_ref[...] = jnp.zeros(ls_ref.shape, F32)
    acc_ref[...] = jnp.zeros(acc_ref.shape, F32)

    def body(c, carry):
        r0 = pl.multiple_of(c * tk, tk)
        _shifted_step(q, k_ref[0, 0, pl.ds(r0, tk), :], v_ref[0, 0, :, pl.ds(r0, tk)], bound, ls_ref, acc_ref)
        return carry

    lax.fori_loop(0, S // tk, body, 0, unroll=min(4, S // tk))
    ot = acc_ref[...] / jnp.sum(ls_ref[...], axis=0, keepdims=True)
    o_ref[0] = ot.T.astype(o_ref.dtype)


def _diff_shifted_kernel(b_ref, q_ref, k_ref, v_ref, lam_ref, gsub_ref, o_ref,
                         ls1_ref, a1_ref, ls2_ref, a2_ref, *, tk, lam_init):
    S = k_ref.shape[2]
    q1, q2 = _split_maps(q_ref[0, 0])
    bound = b_ref[0]
    for r in (ls1_ref, a1_ref, ls2_ref, a2_ref):
        r[...] = jnp.zeros(r.shape, F32)

    def body(c, carry):
        r0 = pl.multiple_of(c * tk, tk)
        ks = k_ref[0, 0, pl.ds(r0, tk), :]
        vt = v_ref[0, 0, :, pl.ds(r0, tk)]
        _shifted_step(q1, ks, vt, bound, ls1_ref, a1_ref)
        _shifted_step(q2, ks, vt, bound, ls2_ref, a2_ref)
        return carry

    lax.fori_loop(0, S // tk, body, 0, unroll=min(4, S // tk))
    o1 = a1_ref[...] / jnp.sum(ls1_ref[...], axis=0, keepdims=True)
    o2 = a2_ref[...] / jnp.sum(ls2_ref[...], axis=0, keepdims=True)
    o = (o1 - _lambda(lam_ref, lam_init) * o2).T
    o = _rms(o, DIFF_DV) * gsub_ref[...] * (1.0 - lam_init)
    o_ref[0] = o.astype(o_ref.dtype)


def _attn_call(q, k, v, extra, bound, *, tq, tk, lam_init=None):
    B, H, S, dqk = q.shape
    dv = v.shape[2]
    tk_safe = min(tk, 512)
    stat = pltpu.VMEM((tq, 1), F32)
    acc = pltpu.VMEM((tq, dv), F32)
    part_t = pltpu.VMEM((8, tq), F32)
    acc_t = pltpu.VMEM((dv, tq), F32)
    qkv_specs = [
        pl.BlockSpec((1, 1, tq, dqk), lambda b, h, i: (b, h, i, 0)),
        pl.BlockSpec((1, 1, S, dqk), lambda b, h, i: (b, h, 0, 0)),
        pl.BlockSpec((1, 1, dv, S), lambda b, h, i: (b, h, 0, 0)),
    ]
    extra_specs = [pl.BlockSpec(e.shape, lambda b, h, i: (0, 0)) for e in extra]
    smem = pl.BlockSpec(memory_space=pltpu.SMEM)
    if lam_init is None:
        safe_kern = functools.partial(_mla_attn_kernel, tk=tk_safe)
        fast_kern = functools.partial(_mla_shifted_kernel, tk=tk)
        safe_scratch, fast_scratch = [stat, stat, acc], [part_t, acc_t]
        name = "mla_attn"
    else:
        safe_kern = functools.partial(_diff_attn_kernel, tk=tk_safe, lam_init=lam_init)
        fast_kern = functools.partial(_diff_shifted_kernel, tk=tk, lam_init=lam_init)
        safe_scratch, fast_scratch = [stat, stat, acc, stat, stat, acc], [part_t, acc_t, part_t, acc_t]
        name = "diff_attn"
    common = dict(
        grid=(B, H, S // tq),
        out_specs=pl.BlockSpec((1, tq, dv), lambda b, h, i: (b, i, h)),
        out_shape=jax.ShapeDtypeStruct((B, S, H * dv), BF16),
        compiler_params=_params(("arbitrary", "arbitrary", "arbitrary")),
    )

    def fast(bound, q, k, v, *extra):
        return pl.pallas_call(fast_kern, in_specs=[smem] + qkv_specs + extra_specs, scratch_shapes=fast_scratch,
                              name=name + "_shifted", **common)((bound * LOG2E).reshape(1), q, k, v, *extra)

    def safe(bound, q, k, v, *extra):
        return pl.pallas_call(safe_kern, in_specs=qkv_specs + extra_specs, scratch_shapes=safe_scratch,
                              name=name, **common)(q, k, v, *extra)

    return lax.cond(bound <= MAX_SHIFT_BOUND, fast, safe, bound, q, k, v, *extra)


def _conv_kernel(prev_ref, cur_ref, next_ref, w_ref, b_ref, g_ref, beta_ref, o_ref, ext_ref, *, rows):
    i = pl.program_id(1)
    ni = pl.num_programs(1)
    tm = cur_ref.shape[1]
    halo = CONV_HALO
    prev = prev_ref[0]
    nxt = next_ref[0]
    ext_ref[0, 0:halo, :] = jnp.where(i > 0, prev, jnp.zeros_like(prev))
    ext_ref[0, halo:halo + tm, :] = cur_ref[0]
    ext_ref[0, halo + tm:2 * halo + tm, :] = jnp.where(i < ni - 1, nxt, jnp.zeros_like(nxt))
    off = halo - CONV_WIDTH // 2
    span = tm + 8 * ((off + CONV_WIDTH - 1) // 8)
    for s in range(1, 8):
        for c0 in range(0, span, rows):
            n = min(rows, span - c0)
            ext_ref[s, c0:c0 + n, :] = ext_ref[0, c0 + s:c0 + s + n, :]
    w = w_ref[...]
    bias = b_ref[...]
    g = g_ref[...]
    beta = beta_ref[...]

    def body(r, carry):
        r0 = pl.multiple_of(r * rows, rows)
        acc = jnp.zeros((rows, CONV_CH), F32)
        for j in range(CONV_WIDTH):
            a, s = divmod(j + off, 8)
            acc = acc + ext_ref[s, pl.ds(r0 + 8 * a, rows), :] * w[j:j + 1, :]
        acc = acc + bias
        mu = jnp.mean(acc, axis=-1, keepdims=True)
        d = acc - mu
        var = jnp.mean(d * d, axis=-1, keepdims=True)
        y = d * lax.rsqrt(var + EPS) * g + beta
        o_ref[0, pl.ds(r0, rows), :] = (y * jax.nn.sigmoid(y)).astype(o_ref.dtype)
        return carry

    lax.fori_loop(0, tm // rows, body, 0)


def _conv_call(glu, w, b, g, beta, *, tm, rows=64):
    B, S, C = glu.shape
    hb = tm // CONV_HALO
    nh = S // CONV_HALO
    const = lambda shape: pl.BlockSpec(shape, lambda b_, i: (0, 0))
    return pl.pallas_call(
        functools.partial(_conv_kernel, rows=min(rows, tm)),
        grid=(B, S // tm),
        in_specs=[
            pl.BlockSpec((1, CONV_HALO, C), lambda b_, i: (b_, jnp.maximum(i * hb - 1, 0), 0)),
            pl.BlockSpec((1, tm, C), lambda b_, i: (b_, i, 0)),
            pl.BlockSpec((1, CONV_HALO, C), lambda b_, i: (b_, jnp.minimum((i + 1) * hb, nh - 1), 0)),
            const(w.shape), const((1, C)), const((1, C)), const((1, C)),
        ],
        out_specs=pl.BlockSpec((1, tm, C), lambda b_, i: (b_, i, 0)),
        out_shape=jax.ShapeDtypeStruct((B, S, C), BF16),
        scratch_shapes=[pltpu.VMEM((8, tm + 2 * CONV_HALO, C), F32)],
        compiler_params=_params(("arbitrary", "arbitrary")),
        name="conv",
    )(glu, glu, glu, w, b, g, beta)


def _out_kernel(x_ref, mod_ref, mla_ref, dif_ref, cnv_ref, w_ref, o_ref):
    n1 = mla_ref.shape[-1]
    n2 = n1 + dif_ref.shape[-1]
    acc = (_dot(mla_ref[0], w_ref[0:n1, :]) + _dot(dif_ref[0], w_ref[n1:n2, :])
           + _dot(cnv_ref[0], w_ref[n2:, :]))
    o_ref[0] = x_ref[0] + mod_ref[0, 5:6, :] * acc


def _out_call(x, mod, o_mla, o_dif, o_cnv, w_out, *, tm):
    B, S, D = x.shape
    tok = lambda w: pl.BlockSpec((1, tm, w), lambda b, i: (b, i, 0))
    return pl.pallas_call(
        _out_kernel,
        grid=(B, S // tm),
        in_specs=[tok(D), pl.BlockSpec((1, N_MOD, D), lambda b, i: (b, 0, 0)),
                  tok(o_mla.shape[-1]), tok(o_dif.shape[-1]), tok(o_cnv.shape[-1]),
                  pl.BlockSpec(w_out.shape, lambda b, i: (0, 0))],
        out_specs=tok(D),
        out_shape=jax.ShapeDtypeStruct((B, S, D), F32),
        compiler_params=_params(("arbitrary", "arbitrary")),
        name="out",
    )(x, mod, o_mla, o_dif, o_cnv, w_out)


def _rope_tables(S):
    half = DIFF_DK // 2
    inv_freq = jnp.float32(ROPE_THETA) ** (-jnp.arange(half, dtype=F32) * (2.0 / DIFF_DK))
    ang = jnp.arange(S, dtype=jnp.int32).astype(F32)[:, None] * inv_freq[None, :]
    c, s, z = jnp.cos(ang), jnp.sin(ang), jnp.zeros_like(ang)
    return (jnp.concatenate([c, c, c, c], axis=1),
            jnp.concatenate([-s, z, -s, z], axis=1),
            jnp.concatenate([z, s, z, s], axis=1))


def _layer_weights(l, w_in, g_cq, g_ckv, w_uq, w_ukv, g_mla_q, g_mla_k, g_diff_q, g_diff_k,
                   lam_q1, lam_k1, lam_q2, lam_k2, g_diff_sub, w_dw, b_dw, g_conv, b_conv, w_out):
    wi = w_in[l]
    o_rope = MLA_Q_RANK + MLA_KV_RANK
    o_dq = o_rope + MLA_ROPE
    w_in_p = jnp.concatenate(
        [wi[:, :o_dq], jnp.zeros((D_MODEL, LANE - MLA_ROPE), wi.dtype), wi[:, o_dq:]], axis=1).astype(BF16)
    w_uq_p = jnp.pad(w_uq[l].reshape(MLA_Q_RANK, MLA_HEADS, MLA_QK),
                     ((0, 0), (0, 0), (0, MLA_QK_PAD - MLA_QK))).reshape(MLA_Q_RANK, -1).astype(BF16)
    pad_qk = lambda g: jnp.pad(g, (0, MLA_QK_PAD - MLA_QK)).reshape(1, MLA_QK_PAD)
    return dict(
        w_in=w_in_p, g_cq=g_cq[l].reshape(1, -1), g_ckv=g_ckv[l].reshape(1, -1),
        w_uq=w_uq_p, w_ukv=w_ukv[l].astype(BF16),
        g_q=pad_qk(g_mla_q[l] * (LOG2E / math.sqrt(MLA_QK))), g_k=pad_qk(g_mla_k[l]),
        g_dq=jnp.tile(g_diff_q[l] * (LOG2E / math.sqrt(DIFF_DK)), 2 * DIFF_HEADS).reshape(1, -1),
        g_dk=jnp.tile(g_diff_k[l], 2 * DIFF_HEADS).reshape(1, -1),
        bd=jnp.asarray(np.kron(np.eye(GROUP_TILE // DIFF_DK), np.ones((DIFF_DK, DIFF_DK))), BF16),
        lam=jnp.stack([lam_q1[l], lam_k1[l], lam_q2[l], lam_k2[l]]),
        b_mla=1.01 * math.sqrt(MLA_QK) * jnp.max(jnp.abs(g_mla_q[l])) * jnp.max(jnp.abs(g_mla_k[l])),
        b_dif=1.01 * math.sqrt(DIFF_DK) * jnp.max(jnp.abs(g_diff_q[l])) * jnp.max(jnp.abs(g_diff_k[l])),
        g_sub=g_diff_sub[l].reshape(1, -1),
        w_dw=jnp.pad(w_dw[l], ((0, 1), (0, 0))), b_dw=b_dw[l].reshape(1, -1),
        g_conv=g_conv[l].reshape(1, -1), b_conv=b_conv[l].reshape(1, -1),
        w_out=w_out[l].astype(BF16),
    )


def _tile(S, t):
    return min(S, t)


def kernel(x_prompt, x_sample, c_prompt, c_sample, w_ada, b_ada, g_norm, w_ffn1_in, w_ffn1_out, w_ffn2_in, w_ffn2_out, w_in, g_cq, g_ckv, w_uq, w_ukv, g_mla_q, g_mla_k, g_diff_q, g_diff_k, lam_q1, lam_k1, lam_q2, lam_k2, g_diff_sub, w_dw, b_dw, g_conv, b_conv, w_out):
    L = w_ada.shape[0]
    bp, bs = c_prompt.shape[0], c_sample.shape[0]
    rows = -(-(bp + bs) // 8) * 8
    c_all = jnp.pad(jnp.concatenate([c_prompt, c_sample], axis=0), ((0, rows - bp - bs), (0, 0)))
    mod_all = _ada_call(c_all, w_ada, b_ada)

    def chunked_up(w):
        return w.astype(BF16).reshape(L, D_MODEL, 2 * D_FF // FFN_TF, FFN_TF).transpose(0, 2, 1, 3)

    up1, dn1 = chunked_up(w_ffn1_in), w_ffn1_out.astype(BF16)
    up2, dn2 = chunked_up(w_ffn2_in), w_ffn2_out.astype(BF16)
    layers = [_layer_weights(l, w_in, g_cq, g_ckv, w_uq, w_ukv, g_mla_q, g_mla_k, g_diff_q, g_diff_k,
                             lam_q1, lam_k1, lam_q2, lam_k2, g_diff_sub, w_dw, b_dw, g_conv, b_conv, w_out)
              for l in range(L)]

    def trunk(x, row0):
        B, S, D = x.shape
        tm, tm_ffn, tq, tk = (_tile(S, t) for t in (TOKEN_TILE, FFN_TOKEN_TILE, ATTN_Q_TILE, ATTN_K_TILE))
        tabs = _rope_tables(S)
        for l in range(L):
            lw = layers[l]
            mod = mod_all[l, row0:row0 + B].reshape(B, N_MOD, D)
            gn = g_norm[l]
            x = _ffn_call(x, mod, gn[0:1], up1, dn1, gn[3:4], layer=l, sub=0, final_norm=False, tm=tm_ffn)
            q, k, v, dq, dk, dv, glu = _proj_call(x, mod, gn[1:2], lw, tabs, tm=tm)
            o_mla = _attn_call(q, k, v, (), lw["b_mla"], tq=tq, tk=tk)
            lam_init = 0.8 - 0.6 * math.exp(-0.3 * l)
            o_dif = _attn_call(dq, dk, dv, (lw["lam"], lw["g_sub"]), lw["b_dif"], tq=tq, tk=tk, lam_init=lam_init)
            o_cnv = _conv_call(glu, lw["w_dw"], lw["b_dw"], lw["g_conv"], lw["b_conv"], tm=tm)
            x = _out_call(x, mod, o_mla, o_dif, o_cnv, lw["w_out"], tm=tm)
            x = _ffn_call(x, mod, gn[2:3], up2, dn2, gn[3:4], layer=l, sub=2, final_norm=True, tm=tm_ffn)
        return x

    return (trunk(x_prompt, 0), trunk(x_sample, bp))
```

```python
import functools
import math

import numpy as np
import jax
import jax.numpy as jnp
from jax import lax
from jax.experimental import pallas as pl
from jax.experimental.pallas import tpu as pltpu

F32 = jnp.float32
BF16 = jnp.bfloat16

D_MODEL = 2048
MLA_HEADS = 6
MLA_Q_RANK = 512
MLA_KV_RANK = 256
MLA_NOPE = 128
MLA_ROPE = 64
MLA_V = 128
MLA_QK = MLA_NOPE + MLA_ROPE
MLA_QK_PAD = 256
DIFF_HEADS = 6
DIFF_DK = 64
DIFF_DV = 128
CONV_CH = 512
CONV_WIDTH = 31
CONV_HALO = 16
D_FF = 5632
N_MOD = 9
ROPE_THETA = 10000.0
EPS = 1e-6
LANE = 128

SEC_A = (0, 896)
SEC_DQ = (896, 1664)
SEC_DK = (1664, 2432)
SEC_DV = (2432, 3200)
SEC_CU = (3200, 4224)
IN_COLS_PAD = 4224

VMEM_LIMIT = 56 * 1024 * 1024
FFN_VMEM_LIMIT = 60 * 1024 * 1024
MAX_SHIFT_BOUND = 40.0
LOG2E = 1.4426950408889634
TOKEN_TILE = 512
FFN_TOKEN_TILE = 1024
ATTN_Q_TILE = 1024
ATTN_K_TILE = 2048
FFN_TF = 512
GROUP_TILE = 256


def _dot(a, b):
    return jnp.dot(a, b, preferred_element_type=F32)


def _dot_nt(a, b):
    return lax.dot_general(a, b, (((1,), (1,)), ((), ())), preferred_element_type=F32)


def _rms(x, width):
    return x * lax.rsqrt(jnp.sum(x * x, axis=-1, keepdims=True) * (1.0 / width) + EPS)


def _adaln(x, g, sc, sh):
    return (_rms(x, x.shape[-1]) * g) * (1.0 + sc) + sh


def _rope(v, cos, sin_lo, sin_hi):
    return v * cos + pltpu.roll(v, 96, 1) * sin_lo + pltpu.roll(v, 32, 1) * sin_hi


def _params(sem, vmem=VMEM_LIMIT):
    return pltpu.CompilerParams(dimension_semantics=sem, vmem_limit_bytes=vmem)


def _ada_kernel(c_ref, w_ref, b_ref, o_ref):
    c = c_ref[...]
    s = c * jax.nn.sigmoid(c)
    s_hi = s.astype(BF16)
    s_lo = (s - s_hi.astype(F32)).astype(BF16)
    w = w_ref[0]
    w_hi = w.astype(BF16)
    w_lo = (w - w_hi.astype(F32)).astype(BF16)
    o_ref[0] = _dot(s_hi, w_hi) + _dot(s_lo, w_hi) + _dot(s_hi, w_lo) + b_ref[0]


def _ada_call(c_all, w_ada, b_ada, tn=1024):
    L, D, N = w_ada.shape
    R = c_all.shape[0]
    return pl.pallas_call(
        _ada_kernel,
        grid=(L, N // tn),
        in_specs=[
            pl.BlockSpec((R, D), lambda l, n: (0, 0)),
            pl.BlockSpec((1, D, tn), lambda l, n: (l, 0, n)),
            pl.BlockSpec((1, 1, tn), lambda l, n: (l, 0, n)),
        ],
        out_specs=pl.BlockSpec((1, R, tn), lambda l, n: (l, 0, n)),
        out_shape=jax.ShapeDtypeStruct((L, R, N), F32),
        compiler_params=_params(("arbitrary", "arbitrary")),
        name="ada",
    )(c_all, w_ada, b_ada.reshape(L, 1, N))


def _ffn_kernel(x_ref, mod_ref, g_ref, wa_ref, wb_ref, wd_ref, gf_ref, o_ref, h_ref, act_ref, *, sub, final_norm):
    j = pl.program_id(2)
    nf = pl.num_programs(2) - 1

    def gated(h):
        a = _dot(h, wa_ref[...])
        b = _dot(h, wb_ref[...])
        return (a * jax.nn.sigmoid(a) * b).astype(BF16)

    def up(slot):
        act_ref[slot] = gated(h_ref[...])

    def down(slot):
        return _dot(act_ref[slot], wd_ref[...])

    @pl.when(j == 0)
    def _():
        sh = mod_ref[0, 3 * sub:3 * sub + 1, :]
        sc = mod_ref[0, 3 * sub + 1:3 * sub + 2, :]
        h_ref[...] = _adaln(x_ref[0], g_ref[...], sc, sh).astype(BF16)
        o_ref[0] = jnp.zeros(o_ref.shape[1:], F32)
        up(0)

    @pl.when((j > 0) & (j < nf))
    def _():
        up(j % 2)
        o_ref[0] += down((j - 1) % 2)

    @pl.when(j == nf)
    def _():
        ga = mod_ref[0, 3 * sub + 2:3 * sub + 3, :]
        y = x_ref[0] + 0.5 * ga * (o_ref[0] + down((nf - 1) % 2))
        if final_norm:
            y = _rms(y, y.shape[-1]) * gf_ref[...]
        o_ref[0] = y


def _ffn_call(x, mod, g_pre, w_up, w_down, g_fin, *, layer, sub, final_norm, tm):
    B, S, D = x.shape
    tf = FFN_TF
    nf = D_FF // tf
    kern = functools.partial(_ffn_kernel, sub=sub, final_norm=final_norm)
    return pl.pallas_call(
        kern,
        grid=(B, S // tm, nf + 1),
        in_specs=[
            pl.BlockSpec((1, tm, D), lambda b, i, j: (b, i, 0)),
            pl.BlockSpec((1, N_MOD, D), lambda b, i, j: (b, 0, 0)),
            pl.BlockSpec((1, D), lambda b, i, j: (0, 0)),
            pl.BlockSpec((None, D, tf), lambda b, i, j: (layer, 0, jnp.where(j < nf, j, 0))),
            pl.BlockSpec((None, D, tf), lambda b, i, j: (layer, 0, jnp.where(j < nf, j, 0) + nf)),
            pl.BlockSpec((None, tf, D), lambda b, i, j: (layer, jnp.where(j > 0, j - 1, nf - 1), 0)),
            pl.BlockSpec((1, D), lambda b, i, j: (0, 0)),
        ],
        out_specs=pl.BlockSpec((1, tm, D), lambda b, i, j: (b, i, 0)),
        out_shape=jax.ShapeDtypeStruct((B, S, D), F32),
        scratch_shapes=[pltpu.VMEM((tm, D), BF16), pltpu.VMEM((2, tm, tf), BF16)],
        compiler_params=_params(("arbitrary", "arbitrary", "arbitrary"), FFN_VMEM_LIMIT),
        name="ffn",
    )(x, mod, g_pre, w_up, w_up, w_down, g_fin)


def _proj_kernel(x_ref, mod_ref, g_ref, win_ref, gcq_ref, gckv_ref, wuq_ref, wukv_ref, gq_ref, gk_ref,
                 gdq_ref, gdk_ref, bd_ref, cos_ref, slo_ref, shi_ref,
                 q_ref, k_ref, v_ref, dq_ref, dk_ref, dv_ref, glu_ref):
    sh = mod_ref[0, 3:4, :]
    sc = mod_ref[0, 4:5, :]
    h = _adaln(x_ref[0], g_ref[...], sc, sh).astype(BF16)
    cos = cos_ref[...]
    slo = slo_ref[...]
    shi = shi_ref[...]

    pa = _dot(h, win_ref[:, SEC_A[0]:SEC_A[1]])
    cq = pa[:, 0:MLA_Q_RANK]
    ckv = pa[:, MLA_Q_RANK:MLA_Q_RANK + MLA_KV_RANK]
    kr = pa[:, MLA_Q_RANK + MLA_KV_RANK:SEC_A[1]]
    cqn = (_rms(cq, MLA_Q_RANK) * gcq_ref[...]).astype(BF16)
    ckvn = (_rms(ckv, MLA_KV_RANK) * gckv_ref[...]).astype(BF16)
    qraw = _dot(cqn, wuq_ref[...])
    kv = _dot(ckvn, wukv_ref[...])
    kr_ss = jnp.sum(kr * kr, axis=-1, keepdims=True)
    gq = gq_ref[...]
    gk = gk_ref[...]
    for hd in range(MLA_HEADS):
        c0 = hd * MLA_QK_PAD
        qh = qraw[:, c0:c0 + MLA_QK_PAD]
        qn = _rms(qh, MLA_QK) * gq
        q_ref[0, hd, :, 0:LANE] = qn[:, 0:LANE].astype(BF16)
        q_ref[0, hd, :, LANE:2 * LANE] = _rope(qn[:, LANE:2 * LANE], cos, slo, shi).astype(BF16)
        kn = kv[:, c0:c0 + LANE]
        inv = lax.rsqrt((jnp.sum(kn * kn, axis=-1, keepdims=True) + kr_ss) * (1.0 / MLA_QK) + EPS)
        k_ref[0, hd, :, 0:LANE] = (kn * inv * gk[:, 0:LANE]).astype(BF16)
        k_ref[0, hd, :, LANE:2 * LANE] = _rope(kr * inv * gk[:, LANE:2 * LANE], cos, slo, shi).astype(BF16)
        v_ref[0, hd] = kv[:, c0 + LANE:c0 + 2 * LANE].T.astype(BF16)

    bd = bd_ref[...]
    for sec, g_t, o_ref in ((SEC_DQ, gdq_ref, dq_ref), (SEC_DK, gdk_ref, dk_ref)):
        z = _dot(h, win_ref[:, sec[0]:sec[1]])
        g_all = g_t[...]
        for c0 in range(0, z.shape[1], GROUP_TILE):
            zc = z[:, c0:c0 + GROUP_TILE]
            zz = zc * zc
            zz_hi = zz.astype(BF16)
            zz_lo = (zz - zz_hi.astype(F32)).astype(BF16)
            ss = _dot(zz_hi, bd) + _dot(zz_lo, bd)
            zn = zc * lax.rsqrt(ss * (1.0 / DIFF_DK) + EPS) * g_all[:, c0:c0 + GROUP_TILE]
            for t in range(GROUP_TILE // LANE):
                o_ref[0, c0 // LANE + t] = _rope(zn[:, t * LANE:(t + 1) * LANE], cos, slo, shi).astype(BF16)
    z = _dot(h, win_ref[:, SEC_DV[0]:SEC_DV[1]])
    for hd in range(DIFF_HEADS):
        dv_ref[0, hd] = z[:, hd * LANE:(hd + 1) * LANE].T.astype(BF16)

    cu = _dot(h, win_ref[:, SEC_CU[0]:SEC_CU[1]])
    glu_ref[0] = cu[:, 0:CONV_CH] * jax.nn.sigmoid(cu[:, CONV_CH:2 * CONV_CH])


def _proj_call(x, mod, g_pre, lw, rope_tabs, *, tm):
    B, S, D = x.shape
    H = MLA_HEADS
    const = lambda shape: pl.BlockSpec(shape, lambda b, i: (0,) * len(shape))
    tab = pl.BlockSpec((tm, LANE), lambda b, i: (i, 0))
    head_out = lambda w: pl.BlockSpec((1, H, tm, w), lambda b, i: (b, 0, i, 0))
    head_shape = lambda w: jax.ShapeDtypeStruct((B, H, S, w), BF16)
    val_out = pl.BlockSpec((1, H, LANE, tm), lambda b, i: (b, 0, 0, i))
    val_shape = jax.ShapeDtypeStruct((B, H, LANE, S), BF16)
    return pl.pallas_call(
        _proj_kernel,
        grid=(B, S // tm),
        in_specs=[
            pl.BlockSpec((1, tm, D), lambda b, i: (b, i, 0)),
            pl.BlockSpec((1, N_MOD, D), lambda b, i: (b, 0, 0)),
            const((1, D)),
            const((D, IN_COLS_PAD)),
            const((1, MLA_Q_RANK)),
            const((1, MLA_KV_RANK)),
            const((MLA_Q_RANK, H * MLA_QK_PAD)),
            const((MLA_KV_RANK, H * MLA_QK_PAD)),
            const((1, MLA_QK_PAD)),
            const((1, MLA_QK_PAD)),
            const((1, DIFF_HEADS * LANE)),
            const((1, DIFF_HEADS * LANE)),
            const((GROUP_TILE, GROUP_TILE)),
            tab, tab, tab,
        ],
        out_specs=[head_out(MLA_QK_PAD), head_out(MLA_QK_PAD), val_out,
                   head_out(LANE), head_out(LANE), val_out,
                   pl.BlockSpec((1, tm, CONV_CH), lambda b, i: (b, i, 0))],
        out_shape=[head_shape(MLA_QK_PAD), head_shape(MLA_QK_PAD), val_shape,
                   head_shape(LANE), head_shape(LANE), val_shape,
                   jax.ShapeDtypeStruct((B, S, CONV_CH), F32)],
        compiler_params=_params(("arbitrary", "arbitrary")),
        name="proj",
    )(x, mod, g_pre, lw["w_in"], lw["g_cq"], lw["g_ckv"], lw["w_uq"], lw["w_ukv"], lw["g_q"], lw["g_k"],
      lw["g_dq"], lw["g_dk"], lw["bd"], *rope_tabs)


def _softmax_step(s, vt, m_ref, l_ref, acc_ref):
    m_prev = m_ref[...]
    m_new = jnp.maximum(m_prev, jnp.max(s, axis=-1, keepdims=True))
    alpha = jnp.exp2(m_prev - m_new)
    p = jnp.exp2(s - m_new)
    l_ref[...] = alpha * l_ref[...] + jnp.sum(p, axis=-1, keepdims=True)
    acc_ref[...] = alpha * acc_ref[...] + _dot_nt(p.astype(BF16), vt)
    m_ref[...] = m_new


def _split_maps(q):
    lane = lax.broadcasted_iota(jnp.int32, q.shape, 1)
    zero = jnp.zeros_like(q)
    return jnp.where(lane < DIFF_DK, q, zero), jnp.where(lane >= DIFF_DK, q, zero)


def _lambda(lam_ref, lam_init):
    lv = lam_ref[...]
    return (jnp.exp(jnp.sum(lv[0:1] * lv[1:2], axis=-1, keepdims=True))
            - jnp.exp(jnp.sum(lv[2:3] * lv[3:4], axis=-1, keepdims=True)) + lam_init)


def _mla_attn_kernel(q_ref, k_ref, v_ref, o_ref, m_ref, l_ref, acc_ref, *, tk):
    S = k_ref.shape[2]
    q = q_ref[0, 0]
    m_ref[...] = jnp.full(m_ref.shape, -jnp.inf, F32)
    l_ref[...] = jnp.zeros(l_ref.shape, F32)
    acc_ref[...] = jnp.zeros(acc_ref.shape, F32)

    def body(c, carry):
        r0 = pl.multiple_of(c * tk, tk)
        s = _dot_nt(q, k_ref[0, 0, pl.ds(r0, tk), :])
        _softmax_step(s, v_ref[0, 0, :, pl.ds(r0, tk)], m_ref, l_ref, acc_ref)
        return carry

    lax.fori_loop(0, S // tk, body, 0)
    o_ref[0] = (acc_ref[...] / l_ref[...]).astype(o_ref.dtype)


def _diff_attn_kernel(q_ref, k_ref, v_ref, lam_ref, gsub_ref, o_ref,
                      m1_ref, l1_ref, a1_ref, m2_ref, l2_ref, a2_ref, *, tk, lam_init):
    S = k_ref.shape[2]
    q1, q2 = _split_maps(q_ref[0, 0])
    for m_ref, l_ref, a_ref in ((m1_ref, l1_ref, a1_ref), (m2_ref, l2_ref, a2_ref)):
        m_ref[...] = jnp.full(m_ref.shape, -jnp.inf, F32)
        l_ref[...] = jnp.zeros(l_ref.shape, F32)
        a_ref[...] = jnp.zeros(a_ref.shape, F32)

    def body(c, carry):
        r0 = pl.multiple_of(c * tk, tk)
        ks = k_ref[0, 0, pl.ds(r0, tk), :]
        vt = v_ref[0, 0, :, pl.ds(r0, tk)]
        _softmax_step(_dot_nt(q1, ks), vt, m1_ref, l1_ref, a1_ref)
        _softmax_step(_dot_nt(q2, ks), vt, m2_ref, l2_ref, a2_ref)
        return carry

    lax.fori_loop(0, S // tk, body, 0)
    o = a1_ref[...] / l1_ref[...] - _lambda(lam_ref, lam_init) * (a2_ref[...] / l2_ref[...])
    o = _rms(o, DIFF_DV) * gsub_ref[...] * (1.0 - lam_init)
    o_ref[0] = o.astype(o_ref.dtype)


def _shifted_step(q, ks, vt, bound, ls_ref, acc_ref):
    pt = jnp.exp2(_dot_nt(ks, q) - bound)
    part = pt[0:8, :]
    for t in range(1, pt.shape[0] // 8):
        part = part + pt[8 * t:8 * t + 8, :]
    ls_ref[...] += part
    acc_ref[...] += _dot(vt, pt.astype(BF16))


def _mla_shifted_kernel(b_ref, q_ref, k_ref, v_ref, o_ref, ls_ref, acc_ref, *, tk):
    S = k_ref.shape[2]
    q = q_ref[0, 0]
    bound = b_ref[0]
    ls_ref[...] = jnp.zeros(ls_ref.shape, F32)
    acc_ref[...] = jnp.zeros(acc_ref.shape, F32)

    def body(c, carry):
        r0 = pl.multiple_of(c * tk, tk)
        _shifted_step(q, k_ref[0, 0, pl.ds(r0, tk), :], v_ref[0, 0, :, pl.ds(r0, tk)], bound, ls_ref, acc_ref)
        return carry

    lax.fori_loop(0, S // tk, body, 0, unroll=min(4, S // tk))
    ot = acc_ref[...] / jnp.sum(ls_ref[...], axis=0, keepdims=True)
    o_ref[0] = ot.T.astype(o_ref.dtype)


def _diff_shifted_kernel(b_ref, q_ref, k_ref, v_ref, lam_ref, gsub_ref, o_ref,
                         ls1_ref, a1_ref, ls2_ref, a2_ref, *, tk, lam_init):
    S = k_ref.shape[2]
    q1, q2 = _split_maps(q_ref[0, 0])
    bound = b_ref[0]
    for r in (ls1_ref, a1_ref, ls2_ref, a2_ref):
        r[...] = jnp.zeros(r.shape, F32)

    def body(c, carry):
        r0 = pl.multiple_of(c * tk, tk)
        ks = k_ref[0, 0, pl.ds(r0, tk), :]
        vt = v_ref[0, 0, :, pl.ds(r0, tk)]
        _shifted_step(q1, ks, vt, bound, ls1_ref, a1_ref)
        _shifted_step(q2, ks, vt, bound, ls2_ref, a2_ref)
        return carry

    lax.fori_loop(0, S // tk, body, 0, unroll=min(4, S // tk))
    o1 = a1_ref[...] / jnp.sum(ls1_ref[...], axis=0, keepdims=True)
    o2 = a2_ref[...] / jnp.sum(ls2_ref[...], axis=0, keepdims=True)
    o = (o1 - _lambda(lam_ref, lam_init) * o2).T
    o = _rms(o, DIFF_DV) * gsub_ref[...] * (1.0 - lam_init)
    o_ref[0] = o.astype(o_ref.dtype)


def _attn_call(q, k, v, extra, bound, *, tq, tk, lam_init=None):
    B, H, S, dqk = q.shape
    dv = v.shape[2]
    tk_safe = min(tk, 512)
    stat = pltpu.VMEM((tq, 1), F32)
    acc = pltpu.VMEM((tq, dv), F32)
    part_t = pltpu.VMEM((8, tq), F32)
    acc_t = pltpu.VMEM((dv, tq), F32)
    qkv_specs = [
        pl.BlockSpec((1, 1, tq, dqk), lambda b, h, i: (b, h, i, 0)),
        pl.BlockSpec((1, 1, S, dqk), lambda b, h, i: (b, h, 0, 0)),
        pl.BlockSpec((1, 1, dv, S), lambda b, h, i: (b, h, 0, 0)),
    ]
    extra_specs = [pl.BlockSpec(e.shape, lambda b, h, i: (0, 0)) for e in extra]
    smem = pl.BlockSpec(memory_space=pltpu.SMEM)
    if lam_init is None:
        safe_kern = functools.partial(_mla_attn_kernel, tk=tk_safe)
        fast_kern = functools.partial(_mla_shifted_kernel, tk=tk)
        safe_scratch, fast_scratch = [stat, stat, acc], [part_t, acc_t]
        name = "mla_attn"
    else:
        safe_kern = functools.partial(_diff_attn_kernel, tk=tk_safe, lam_init=lam_init)
        fast_kern = functools.partial(_diff_shifted_kernel, tk=tk, lam_init=lam_init)
        safe_scratch, fast_scratch = [stat, stat, acc, stat, stat, acc], [part_t, acc_t, part_t, acc_t]
        name = "diff_attn"
    common = dict(
        grid=(B, H, S // tq),
        out_specs=pl.BlockSpec((1, tq, dv), lambda b, h, i: (b, i, h)),
        out_shape=jax.ShapeDtypeStruct((B, S, H * dv), BF16),
        compiler_params=_params(("arbitrary", "arbitrary", "arbitrary")),
    )

    def fast(bound, q, k, v, *extra):
        return pl.pallas_call(fast_kern, in_specs=[smem] + qkv_specs + extra_specs, scratch_shapes=fast_scratch,
                              name=name + "_shifted", **common)((bound * LOG2E).reshape(1), q, k, v, *extra)

    def safe(bound, q, k, v, *extra):
        return pl.pallas_call(safe_kern, in_specs=qkv_specs + extra_specs, scratch_shapes=safe_scratch,
                              name=name, **common)(q, k, v, *extra)

    return lax.cond(bound <= MAX_SHIFT_BOUND, fast, safe, bound, q, k, v, *extra)


def _conv_kernel(prev_ref, cur_ref, next_ref, w_ref, b_ref, g_ref, beta_ref, o_ref, ext_ref, *, rows):
    i = pl.program_id(1)
    ni = pl.num_programs(1)
    tm = cur_ref.shape[1]
    halo = CONV_HALO
    prev = prev_ref[0]
    nxt = next_ref[0]
    ext_ref[0, 0:halo, :] = jnp.where(i > 0, prev, jnp.zeros_like(prev))
    ext_ref[0, halo:halo + tm, :] = cur_ref[0]
    ext_ref[0, halo + tm:2 * halo + tm, :] = jnp.where(i < ni - 1, nxt, jnp.zeros_like(nxt))
    off = halo - CONV_WIDTH // 2
    span = tm + 8 * ((off + CONV_WIDTH - 1) // 8)
    for s in range(1, 8):
        for c0 in range(0, span, rows):
            n = min(rows, span - c0)
            ext_ref[s, c0:c0 + n, :] = ext_ref[0, c0 + s:c0 + s + n, :]
    w = w_ref[...]
    bias = b_ref[...]
    g = g_ref[...]
    beta = beta_ref[...]

    def body(r, carry):
        r0 = pl.multiple_of(r * rows, rows)
        acc = jnp.zeros((rows, CONV_CH), F32)
        for j in range(CONV_WIDTH):
            a, s = divmod(j + off, 8)
            acc = acc + ext_ref[s, pl.ds(r0 + 8 * a, rows), :] * w[j:j + 1, :]
        acc = acc + bias
        mu = jnp.mean(acc, axis=-1, keepdims=True)
        d = acc - mu
        var = jnp.mean(d * d, axis=-1, keepdims=True)
        y = d * lax.rsqrt(var + EPS) * g + beta
        o_ref[0, pl.ds(r0, rows), :] = (y * jax.nn.sigmoid(y)).astype(o_ref.dtype)
        return carry

    lax.fori_loop(0, tm // rows, body, 0, unroll=2)


def _conv_call(glu, w, b, g, beta, *, tm, rows=64):
    B, S, C = glu.shape
    hb = tm // CONV_HALO
    nh = S // CONV_HALO
    const = lambda shape: pl.BlockSpec(shape, lambda b_, i: (0, 0))
    return pl.pallas_call(
        functools.partial(_conv_kernel, rows=min(rows, tm)),
        grid=(B, S // tm),
        in_specs=[
            pl.BlockSpec((1, CONV_HALO, C), lambda b_, i: (b_, jnp.maximum(i * hb - 1, 0), 0)),
            pl.BlockSpec((1, tm, C), lambda b_, i: (b_, i, 0)),
            pl.BlockSpec((1, CONV_HALO, C), lambda b_, i: (b_, jnp.minimum((i + 1) * hb, nh - 1), 0)),
            const(w.shape), const((1, C)), const((1, C)), const((1, C)),
        ],
        out_specs=pl.BlockSpec((1, tm, C), lambda b_, i: (b_, i, 0)),
        out_shape=jax.ShapeDtypeStruct((B, S, C), BF16),
        scratch_shapes=[pltpu.VMEM((8, tm + 2 * CONV_HALO, C), F32)],
        compiler_params=_params(("arbitrary", "arbitrary")),
        name="conv",
    )(glu, glu, glu, w, b, g, beta)


def _out_kernel(x_ref, mod_ref, mla_ref, dif_ref, cnv_ref, w_ref, o_ref):
    n1 = mla_ref.shape[-1]
    n2 = n1 + dif_ref.shape[-1]
    acc = (_dot(mla_ref[0], w_ref[0:n1, :]) + _dot(dif_ref[0], w_ref[n1:n2, :])
           + _dot(cnv_ref[0], w_ref[n2:, :]))
    o_ref[0] = x_ref[0] + mod_ref[0, 5:6, :] * acc


def _out_call(x, mod, o_mla, o_dif, o_cnv, w_out, *, tm):
    B, S, D = x.shape
    tok = lambda w: pl.BlockSpec((1, tm, w), lambda b, i: (b, i, 0))
    return pl.pallas_call(
        _out_kernel,
        grid=(B, S // tm),
        in_specs=[tok(D), pl.BlockSpec((1, N_MOD, D), lambda b, i: (b, 0, 0)),
                  tok(o_mla.shape[-1]), tok(o_dif.shape[-1]), tok(o_cnv.shape[-1]),
                  pl.BlockSpec(w_out.shape, lambda b, i: (0, 0))],
        out_specs=tok(D),
        out_shape=jax.ShapeDtypeStruct((B, S, D), F32),
        compiler_params=_params(("arbitrary", "arbitrary")),
        name="out",
    )(x, mod, o_mla, o_dif, o_cnv, w_out)


def _rope_tables(S):
    half = DIFF_DK // 2
    inv_freq = jnp.float32(ROPE_THETA) ** (-jnp.arange(half, dtype=F32) * (2.0 / DIFF_DK))
    ang = jnp.arange(S, dtype=jnp.int32).astype(F32)[:, None] * inv_freq[None, :]
    c, s, z = jnp.cos(ang), jnp.sin(ang), jnp.zeros_like(ang)
    return (jnp.concatenate([c, c, c, c], axis=1),
            jnp.concatenate([-s, z, -s, z], axis=1),
            jnp.concatenate([z, s, z, s], axis=1))


def _layer_weights(l, w_in, g_cq, g_ckv, w_uq, w_ukv, g_mla_q, g_mla_k, g_diff_q, g_diff_k,
                   lam_q1, lam_k1, lam_q2, lam_k2, g_diff_sub, w_dw, b_dw, g_conv, b_conv, w_out):
    wi = w_in[l]
    o_rope = MLA_Q_RANK + MLA_KV_RANK
    o_dq = o_rope + MLA_ROPE
    w_in_p = jnp.concatenate(
        [wi[:, :o_dq], jnp.zeros((D_MODEL, LANE - MLA_ROPE), wi.dtype), wi[:, o_dq:]], axis=1).astype(BF16)
    w_uq_p = jnp.pad(w_uq[l].reshape(MLA_Q_RANK, MLA_HEADS, MLA_QK),
                     ((0, 0), (0, 0), (0, MLA_QK_PAD - MLA_QK))).reshape(MLA_Q_RANK, -1).astype(BF16)
    pad_qk = lambda g: jnp.pad(g, (0, MLA_QK_PAD - MLA_QK)).reshape(1, MLA_QK_PAD)
    return dict(
        w_in=w_in_p, g_cq=g_cq[l].reshape(1, -1), g_ckv=g_ckv[l].reshape(1, -1),
        w_uq=w_uq_p, w_ukv=w_ukv[l].astype(BF16),
        g_q=pad_qk(g_mla_q[l] * (LOG2E / math.sqrt(MLA_QK))), g_k=pad_qk(g_mla_k[l]),
        g_dq=jnp.tile(g_diff_q[l] * (LOG2E / math.sqrt(DIFF_DK)), 2 * DIFF_HEADS).reshape(1, -1),
        g_dk=jnp.tile(g_diff_k[l], 2 * DIFF_HEADS).reshape(1, -1),
        bd=jnp.asarray(np.kron(np.eye(GROUP_TILE // DIFF_DK), np.ones((DIFF_DK, DIFF_DK))), BF16),
        lam=jnp.stack([lam_q1[l], lam_k1[l], lam_q2[l], lam_k2[l]]),
        b_mla=1.01 * math.sqrt(MLA_QK) * jnp.max(jnp.abs(g_mla_q[l])) * jnp.max(jnp.abs(g_mla_k[l])),
        b_dif=1.01 * math.sqrt(DIFF_DK) * jnp.max(jnp.abs(g_diff_q[l])) * jnp.max(jnp.abs(g_diff_k[l])),
        g_sub=g_diff_sub[l].reshape(1, -1),
        w_dw=jnp.pad(w_dw[l], ((0, 1), (0, 0))), b_dw=b_dw[l].reshape(1, -1),
        g_conv=g_conv[l].reshape(1, -1), b_conv=b_conv[l].reshape(1, -1),
        w_out=w_out[l].astype(BF16),
    )


def _tile(S, t):
    return min(S, t)


def kernel(x_prompt, x_sample, c_prompt, c_sample, w_ada, b_ada, g_norm, w_ffn1_in, w_ffn1_out, w_ffn2_in, w_ffn2_out, w_in, g_cq, g_ckv, w_uq, w_ukv, g_mla_q, g_mla_k, g_diff_q, g_diff_k, lam_q1, lam_k1, lam_q2, lam_k2, g_diff_sub, w_dw, b_dw, g_conv, b_conv, w_out):
    L = w_ada.shape[0]
    bp, bs = c_prompt.shape[0], c_sample.shape[0]
    rows = -(-(bp + bs) // 8) * 8
    c_all = jnp.pad(jnp.concatenate([c_prompt, c_sample], axis=0), ((0, rows - bp - bs), (0, 0)))
    mod_all = _ada_call(c_all, w_ada, b_ada)

    up1, dn1 = w_ffn1_in.astype(BF16), w_ffn1_out.astype(BF16)
    up2, dn2 = w_ffn2_in.astype(BF16), w_ffn2_out.astype(BF16)
    layers = [_layer_weights(l, w_in, g_cq, g_ckv, w_uq, w_ukv, g_mla_q, g_mla_k, g_diff_q, g_diff_k,
                             lam_q1, lam_k1, lam_q2, lam_k2, g_diff_sub, w_dw, b_dw, g_conv, b_conv, w_out)
              for l in range(L)]

    def trunk(x, row0):
        B, S, D = x.shape
        tm, tm_ffn, tq, tk = (_tile(S, t) for t in (TOKEN_TILE, FFN_TOKEN_TILE, ATTN_Q_TILE, ATTN_K_TILE))
        tabs = _rope_tables(S)
        for l in range(L):
            lw = layers[l]
            mod = mod_all[l, row0:row0 + B].reshape(B, N_MOD, D)
            gn = g_norm[l]
            x = _ffn_call(x, mod, gn[0:1], up1, dn1, gn[3:4], layer=l, sub=0, final_norm=False, tm=tm_ffn)
            q, k, v, dq, dk, dv, glu = _proj_call(x, mod, gn[1:2], lw, tabs, tm=tm)
            o_mla = _attn_call(q, k, v, (), lw["b_mla"], tq=tq, tk=tk)
            lam_init = 0.8 - 0.6 * math.exp(-0.3 * l)
            o_dif = _attn_call(dq, dk, dv, (lw["lam"], lw["g_sub"]), lw["b_dif"], tq=tq, tk=tk, lam_init=lam_init)
            o_cnv = _conv_call(glu, lw["w_dw"], lw["b_dw"], lw["g_conv"], lw["b_conv"], tm=tm)
            x = _out_call(x, mod, o_mla, o_dif, o_cnv, lw["w_out"], tm=tm)
            x = _ffn_call(x, mod, gn[2:3], up2, dn2, gn[3:4], layer=l, sub=2, final_norm=True, tm=tm_ffn)
        return x

    return (trunk(x_prompt, 0), trunk(x_sample, bp))
```

```python
import functools
import math

import numpy as np
import jax
import jax.numpy as jnp
from jax import lax
from jax.experimental import pallas as pl
from jax.experimental.pallas import tpu as pltpu

F32 = jnp.float32
BF16 = jnp.bfloat16

D_MODEL = 2048
MLA_HEADS = 6
MLA_Q_RANK = 512
MLA_KV_RANK = 256
MLA_NOPE = 128
MLA_ROPE = 64
MLA_V = 128
MLA_QK = MLA_NOPE + MLA_ROPE
MLA_QK_PAD = 256
DIFF_HEADS = 6
DIFF_DK = 64
DIFF_DV = 128
CONV_CH = 512
CONV_WIDTH = 31
CONV_HALO = 16
D_FF = 5632
N_MOD = 9
ROPE_THETA = 10000.0
EPS = 1e-6
LANE = 128

SEC_A = (0, 896)
SEC_DQ = (896, 1664)
SEC_DK = (1664, 2432)
SEC_DV = (2432, 3200)
SEC_CU = (3200, 4224)
IN_COLS_PAD = 4224

VMEM_LIMIT = 56 * 1024 * 1024
FFN_VMEM_LIMIT = 60 * 1024 * 1024
MAX_SHIFT_BOUND = 40.0
LOG2E = 1.4426950408889634
TOKEN_TILE = 512
FFN_TOKEN_TILE = 1024
ATTN_Q_TILE = 1024
ATTN_K_TILE = 2048
FFN_TF = 512
GROUP_TILE = 256


def _dot(a, b):
    return jnp.dot(a, b, preferred_element_type=F32)


def _dot_nt(a, b):
    return lax.dot_general(a, b, (((1,), (1,)), ((), ())), preferred_element_type=F32)


def _rms(x, width):
    return x * lax.rsqrt(jnp.sum(x * x, axis=-1, keepdims=True) * (1.0 / width) + EPS)


def _adaln(x, g, sc, sh):
    return (_rms(x, x.shape[-1]) * g) * (1.0 + sc) + sh


def _rope(v, cos, sin_lo, sin_hi):
    return v * cos + pltpu.roll(v, 96, 1) * sin_lo + pltpu.roll(v, 32, 1) * sin_hi


def _params(sem, vmem=VMEM_LIMIT):
    return pltpu.CompilerParams(dimension_semantics=sem, vmem_limit_bytes=vmem)


def _ada_kernel(c_ref, w_ref, b_ref, o_ref):
    c = c_ref[...]
    s = c * jax.nn.sigmoid(c)
    s_hi = s.astype(BF16)
    s_lo = (s - s_hi.astype(F32)).astype(BF16)
    w = w_ref[0]
    w_hi = w.astype(BF16)
    w_lo = (w - w_hi.astype(F32)).astype(BF16)
    o_ref[0] = _dot(s_hi, w_hi) + _dot(s_lo, w_hi) + _dot(s_hi, w_lo) + b_ref[0]


def _ada_call(c_all, w_ada, b_ada, tn=1024):
    L, D, N = w_ada.shape
    R = c_all.shape[0]
    return pl.pallas_call(
        _ada_kernel,
        grid=(L, N // tn),
        in_specs=[
            pl.BlockSpec((R, D), lambda l, n: (0, 0)),
            pl.BlockSpec((1, D, tn), lambda l, n: (l, 0, n)),
            pl.BlockSpec((1, 1, tn), lambda l, n: (l, 0, n)),
        ],
        out_specs=pl.BlockSpec((1, R, tn), lambda l, n: (l, 0, n)),
        out_shape=jax.ShapeDtypeStruct((L, R, N), F32),
        compiler_params=_params(("arbitrary", "arbitrary")),
        name="ada",
    )(c_all, w_ada, b_ada.reshape(L, 1, N))


def _ffn_kernel(x_ref, mod_ref, g_ref, wa_ref, wb_ref, wd_ref, gf_ref, o_ref, h_ref, act_ref, *, sub, final_norm):
    j = pl.program_id(2)
    nf = pl.num_programs(2) - 1

    def gated(h):
        a = _dot(h, wa_ref[...])
        b = _dot(h, wb_ref[...])
        return (a * jax.nn.sigmoid(a) * b).astype(BF16)

    def up(slot):
        act_ref[slot] = gated(h_ref[...])

    def down(slot):
        return _dot(act_ref[slot], wd_ref[...])

    @pl.when(j == 0)
    def _():
        sh = mod_ref[0, 3 * sub:3 * sub + 1, :]
        sc = mod_ref[0, 3 * sub + 1:3 * sub + 2, :]
        h_ref[...] = _adaln(x_ref[0], g_ref[...], sc, sh).astype(BF16)
        o_ref[0] = jnp.zeros(o_ref.shape[1:], F32)
        up(0)

    @pl.when((j > 0) & (j < nf))
    def _():
        up(j % 2)
        o_ref[0] += down((j - 1) % 2)

    @pl.when(j == nf)
    def _():
        ga = mod_ref[0, 3 * sub + 2:3 * sub + 3, :]
        y = x_ref[0] + 0.5 * ga * (o_ref[0] + down((nf - 1) % 2))
        if final_norm:
            y = _rms(y, y.shape[-1]) * gf_ref[...]
        o_ref[0] = y


def _ffn_call(x, mod, g_pre, w_up, w_down, g_fin, *, layer, sub, final_norm, tm):
    B, S, D = x.shape
    tf = FFN_TF
    nf = D_FF // tf
    kern = functools.partial(_ffn_kernel, sub=sub, final_norm=final_norm)
    return pl.pallas_call(
        kern,
        grid=(B, S // tm, nf + 1),
        in_specs=[
            pl.BlockSpec((1, tm, D), lambda b, i, j: (b, i, 0)),
            pl.BlockSpec((1, N_MOD, D), lambda b, i, j: (b, 0, 0)),
            pl.BlockSpec((1, D), lambda b, i, j: (0, 0)),
            pl.BlockSpec((None, D, tf), lambda b, i, j: (layer, 0, jnp.where(j < nf, j, 0))),
            pl.BlockSpec((None, D, tf), lambda b, i, j: (layer, 0, jnp.where(j < nf, j, 0) + nf)),
            pl.BlockSpec((None, tf, D), lambda b, i, j: (layer, jnp.where(j > 0, j - 1, nf - 1), 0)),
            pl.BlockSpec((1, D), lambda b, i, j: (0, 0)),
        ],
        out_specs=pl.BlockSpec((1, tm, D), lambda b, i, j: (b, i, 0)),
        out_shape=jax.ShapeDtypeStruct((B, S, D), F32),
        scratch_shapes=[pltpu.VMEM((tm, D), BF16), pltpu.VMEM((2, tm, tf), BF16)],
        compiler_params=_params(("arbitrary", "arbitrary", "arbitrary"), FFN_VMEM_LIMIT),
        name="ffn",
    )(x, mod, g_pre, w_up, w_up, w_down, g_fin)


def _proj_kernel(x_ref, mod_ref, g_ref, win_ref, gcq_ref, gckv_ref, wuq_ref, wukv_ref, gq_ref, gk_ref,
                 gdq_ref, gdk_ref, bd_ref, cos_ref, slo_ref, shi_ref,
                 q_ref, k_ref, v_ref, dq_ref, dk_ref, dv_ref, glu_ref):
    sh = mod_ref[0, 3:4, :]
    sc = mod_ref[0, 4:5, :]
    h = _adaln(x_ref[0], g_ref[...], sc, sh).astype(BF16)
    cos = cos_ref[...]
    slo = slo_ref[...]
    shi = shi_ref[...]

    pa = _dot(h, win_ref[:, SEC_A[0]:SEC_A[1]])
    cq = pa[:, 0:MLA_Q_RANK]
    ckv = pa[:, MLA_Q_RANK:MLA_Q_RANK + MLA_KV_RANK]
    kr = pa[:, MLA_Q_RANK + MLA_KV_RANK:SEC_A[1]]
    cqn = (_rms(cq, MLA_Q_RANK) * gcq_ref[...]).astype(BF16)
    ckvn = (_rms(ckv, MLA_KV_RANK) * gckv_ref[...]).astype(BF16)
    qraw = _dot(cqn, wuq_ref[...])
    kv = _dot(ckvn, wukv_ref[...])
    kr_ss = jnp.sum(kr * kr, axis=-1, keepdims=True)
    gq = gq_ref[...]
    gk = gk_ref[...]
    for hd in range(MLA_HEADS):
        c0 = hd * MLA_QK_PAD
        qh = qraw[:, c0:c0 + MLA_QK_PAD]
        qn = _rms(qh, MLA_QK) * gq
        q_ref[0, hd, :, 0:LANE] = qn[:, 0:LANE].astype(BF16)
        q_ref[0, hd, :, LANE:2 * LANE] = _rope(qn[:, LANE:2 * LANE], cos, slo, shi).astype(BF16)
        kn = kv[:, c0:c0 + LANE]
        inv = lax.rsqrt((jnp.sum(kn * kn, axis=-1, keepdims=True) + kr_ss) * (1.0 / MLA_QK) + EPS)
        k_ref[0, hd, :, 0:LANE] = (kn * inv * gk[:, 0:LANE]).astype(BF16)
        k_ref[0, hd, :, LANE:2 * LANE] = _rope(kr * inv * gk[:, LANE:2 * LANE], cos, slo, shi).astype(BF16)
        v_ref[0, hd] = kv[:, c0 + LANE:c0 + 2 * LANE].T.astype(BF16)

    bd = bd_ref[...]
    for sec, g_t, o_ref in ((SEC_DQ, gdq_ref, dq_ref), (SEC_DK, gdk_ref, dk_ref)):
        z = _dot(h, win_ref[:, sec[0]:sec[1]])
        g_all = g_t[...]
        for c0 in range(0, z.shape[1], GROUP_TILE):
            zc = z[:, c0:c0 + GROUP_TILE]
            zz = zc * zc
            zz_hi = zz.astype(BF16)
            zz_lo = (zz - zz_hi.astype(F32)).astype(BF16)
            ss = _dot(zz_hi, bd) + _dot(zz_lo, bd)
            zn = zc * lax.rsqrt(ss * (1.0 / DIFF_DK) + EPS) * g_all[:, c0:c0 + GROUP_TILE]
            for t in range(GROUP_TILE // LANE):
                o_ref[0, c0 // LANE + t] = _rope(zn[:, t * LANE:(t + 1) * LANE], cos, slo, shi).astype(BF16)
    z = _dot(h, win_ref[:, SEC_DV[0]:SEC_DV[1]])
    for hd in range(DIFF_HEADS):
        dv_ref[0, hd] = z[:, hd * LANE:(hd + 1) * LANE].T.astype(BF16)

    cu = _dot(h, win_ref[:, SEC_CU[0]:SEC_CU[1]])
    glu_ref[0] = cu[:, 0:CONV_CH] * jax.nn.sigmoid(cu[:, CONV_CH:2 * CONV_CH])


def _proj_call(x, mod, g_pre, lw, rope_tabs, *, tm):
    B, S, D = x.shape
    H = MLA_HEADS
    const = lambda shape: pl.BlockSpec(shape, lambda b, i: (0,) * len(shape))
    tab = pl.BlockSpec((tm, LANE), lambda b, i: (i, 0))
    head_out = lambda w: pl.BlockSpec((1, H, tm, w), lambda b, i: (b, 0, i, 0))
    head_shape = lambda w: jax.ShapeDtypeStruct((B, H, S, w), BF16)
    val_out = pl.BlockSpec((1, H, LANE, tm), lambda b, i: (b, 0, 0, i))
    val_shape = jax.ShapeDtypeStruct((B, H, LANE, S), BF16)
    return pl.pallas_call(
        _proj_kernel,
        grid=(B, S // tm),
        in_specs=[
            pl.BlockSpec((1, tm, D), lambda b, i: (b, i, 0)),
            pl.BlockSpec((1, N_MOD, D), lambda b, i: (b, 0, 0)),
            const((1, D)),
            const((D, IN_COLS_PAD)),
            const((1, MLA_Q_RANK)),
            const((1, MLA_KV_RANK)),
            const((MLA_Q_RANK, H * MLA_QK_PAD)),
            const((MLA_KV_RANK, H * MLA_QK_PAD)),
            const((1, MLA_QK_PAD)),
            const((1, MLA_QK_PAD)),
            const((1, DIFF_HEADS * LANE)),
            const((1, DIFF_HEADS * LANE)),
            const((GROUP_TILE, GROUP_TILE)),
            tab, tab, tab,
        ],
        out_specs=[head_out(MLA_QK_PAD), head_out(MLA_QK_PAD), val_out,
                   head_out(LANE), head_out(LANE), val_out,
                   pl.BlockSpec((1, tm, CONV_CH), lambda b, i: (b, i, 0))],
        out_shape=[head_shape(MLA_QK_PAD), head_shape(MLA_QK_PAD), val_shape,
                   head_shape(LANE), head_shape(LANE), val_shape,
                   jax.ShapeDtypeStruct((B, S, CONV_CH), F32)],
        compiler_params=_params(("arbitrary", "arbitrary")),
        name="proj",
    )(x, mod, g_pre, lw["w_in"], lw["g_cq"], lw["g_ckv"], lw["w_uq"], lw["w_ukv"], lw["g_q"], lw["g_k"],
      lw["g_dq"], lw["g_dk"], lw["bd"], *rope_tabs)


def _softmax_step(s, vt, m_ref, l_ref, acc_ref):
    m_prev = m_ref[...]
    m_new = jnp.maximum(m_prev, jnp.max(s, axis=-1, keepdims=True))
    alpha = jnp.exp2(m_prev - m_new)
    p = jnp.exp2(s - m_new)
    l_ref[...] = alpha * l_ref[...] + jnp.sum(p, axis=-1, keepdims=True)
    acc_ref[...] = alpha * acc_ref[...] + _dot_nt(p.astype(BF16), vt)
    m_ref[...] = m_new


def _split_maps(q):
    lane = lax.broadcasted_iota(jnp.int32, q.shape, 1)
    zero = jnp.zeros_like(q)
    return jnp.where(lane < DIFF_DK, q, zero), jnp.where(lane >= DIFF_DK, q, zero)


def _lambda(lam_ref, lam_init):
    lv = lam_ref[...]
    return (jnp.exp(jnp.sum(lv[0:1] * lv[1:2], axis=-1, keepdims=True))
            - jnp.exp(jnp.sum(lv[2:3] * lv[3:4], axis=-1, keepdims=True)) + lam_init)


def _mla_attn_kernel(q_ref, k_ref, v_ref, o_ref, m_ref, l_ref, acc_ref, *, tk):
    S = k_ref.shape[2]
    q = q_ref[0, 0]
    m_ref[...] = jnp.full(m_ref.shape, -jnp.inf, F32)
    l_ref[...] = jnp.zeros(l_ref.shape, F32)
    acc_ref[...] = jnp.zeros(acc_ref.shape, F32)

    def body(c, carry):
        r0 = pl.multiple_of(c * tk, tk)
        s = _dot_nt(q, k_ref[0, 0, pl.ds(r0, tk), :])
        _softmax_step(s, v_ref[0, 0, :, pl.ds(r0, tk)], m_ref, l_ref, acc_ref)
        return carry

    lax.fori_loop(0, S // tk, body, 0)
    o_ref[0] = (acc_ref[...] / l_ref[...]).astype(o_ref.dtype)


def _diff_attn_kernel(q_ref, k_ref, v_ref, lam_ref, gsub_ref, o_ref,
                      m1_ref, l1_ref, a1_ref, m2_ref, l2_ref, a2_ref, *, tk, lam_init):
    S = k_ref.shape[2]
    q1, q2 = _split_maps(q_ref[0, 0])
    for m_ref, l_ref, a_ref in ((m1_ref, l1_ref, a1_ref), (m2_ref, l2_ref, a2_ref)):
        m_ref[...] = jnp.full(m_ref.shape, -jnp.inf, F32)
        l_ref[...] = jnp.zeros(l_ref.shape, F32)
        a_ref[...] = jnp.zeros(a_ref.shape, F32)

    def body(c, carry):
        r0 = pl.multiple_of(c * tk, tk)
        ks = k_ref[0, 0, pl.ds(r0, tk), :]
        vt = v_ref[0, 0, :, pl.ds(r0, tk)]
        _softmax_step(_dot_nt(q1, ks), vt, m1_ref, l1_ref, a1_ref)
        _softmax_step(_dot_nt(q2, ks), vt, m2_ref, l2_ref, a2_ref)
        return carry

    lax.fori_loop(0, S // tk, body, 0)
    o = a1_ref[...] / l1_ref[...] - _lambda(lam_ref, lam_init) * (a2_ref[...] / l2_ref[...])
    o = _rms(o, DIFF_DV) * gsub_ref[...] * (1.0 - lam_init)
    o_ref[0] = o.astype(o_ref.dtype)


def _shifted_step(q, ks, vt, bound, ls_ref, acc_ref):
    pt = jnp.exp2(_dot_nt(ks, q) - bound)
    part = pt[0:8, :]
    for t in range(1, pt.shape[0] // 8):
        part = part + pt[8 * t:8 * t + 8, :]
    ls_ref[...] += part
    acc_ref[...] += _dot(vt, pt.astype(BF16))


def _mla_shifted_kernel(b_ref, q_ref, k_ref, v_ref, o_ref, ls_ref, acc_ref, *, tk):
    S = k_ref.shape[2]
    q = q_ref[0, 0]
    bound = b_ref[0]
    ls_ref[...] = jnp.zeros(ls_ref.shape, F32)
    acc_ref[...] = jnp.zeros(acc_ref.shape, F32)

    def body(c, carry):
        r0 = pl.multiple_of(c * tk, tk)
        _shifted_step(q, k_ref[0, 0, pl.ds(r0, tk), :], v_ref[0, 0, :, pl.ds(r0, tk)], bound, ls_ref, acc_ref)
        return carry

    lax.fori_loop(0, S // tk, body, 0, unroll=min(8, S // tk))
    ot = acc_ref[...] / jnp.sum(ls_ref[...], axis=0, keepdims=True)
    o_ref[0] = ot.T.astype(o_ref.dtype)


def _diff_shifted_kernel(b_ref, q_ref, k_ref, v_ref, lam_ref, gsub_ref, o_ref,
                         ls1_ref, a1_ref, ls2_ref, a2_ref, *, tk, lam_init):
    S = k_ref.shape[2]
    q1, q2 = _split_maps(q_ref[0, 0])
    bound = b_ref[0]
    for r in (ls1_ref, a1_ref, ls2_ref, a2_ref):
        r[...] = jnp.zeros(r.shape, F32)

    def body(c, carry):
        r0 = pl.multiple_of(c * tk, tk)
        ks = k_ref[0, 0, pl.ds(r0, tk), :]
        vt = v_ref[0, 0, :, pl.ds(r0, tk)]
        _shifted_step(q1, ks, vt, bound, ls1_ref, a1_ref)
        _shifted_step(q2, ks, vt, bound, ls2_ref, a2_ref)
        return carry

    lax.fori_loop(0, S // tk, body, 0, unroll=min(4, S // tk))
    o1 = a1_ref[...] / jnp.sum(ls1_ref[...], axis=0, keepdims=True)
    o2 = a2_ref[...] / jnp.sum(ls2_ref[...], axis=0, keepdims=True)
    o = (o1 - _lambda(lam_ref, lam_init) * o2).T
    o = _rms(o, DIFF_DV) * gsub_ref[...] * (1.0 - lam_init)
    o_ref[0] = o.astype(o_ref.dtype)


def _attn_call(q, k, v, extra, bound, *, tq, tk, lam_init=None):
    B, H, S, dqk = q.shape
    dv = v.shape[2]
    tk_safe = min(tk, 512)
    stat = pltpu.VMEM((tq, 1), F32)
    acc = pltpu.VMEM((tq, dv), F32)
    part_t = pltpu.VMEM((8, tq), F32)
    acc_t = pltpu.VMEM((dv, tq), F32)
    qkv_specs = [
        pl.BlockSpec((1, 1, tq, dqk), lambda b, h, i: (b, h, i, 0)),
        pl.BlockSpec((1, 1, S, dqk), lambda b, h, i: (b, h, 0, 0)),
        pl.BlockSpec((1, 1, dv, S), lambda b, h, i: (b, h, 0, 0)),
    ]
    extra_specs = [pl.BlockSpec(e.shape, lambda b, h, i: (0, 0)) for e in extra]
    smem = pl.BlockSpec(memory_space=pltpu.SMEM)
    if lam_init is None:
        safe_kern = functools.partial(_mla_attn_kernel, tk=tk_safe)
        fast_kern = functools.partial(_mla_shifted_kernel, tk=tk)
        safe_scratch, fast_scratch = [stat, stat, acc], [part_t, acc_t]
        name = "mla_attn"
    else:
        safe_kern = functools.partial(_diff_attn_kernel, tk=tk_safe, lam_init=lam_init)
        fast_kern = functools.partial(_diff_shifted_kernel, tk=tk, lam_init=lam_init)
        safe_scratch, fast_scratch = [stat, stat, acc, stat, stat, acc], [part_t, acc_t, part_t, acc_t]
        name = "diff_attn"
    common = dict(
        grid=(B, H, S // tq),
        out_specs=pl.BlockSpec((1, tq, dv), lambda b, h, i: (b, i, h)),
        out_shape=jax.ShapeDtypeStruct((B, S, H * dv), BF16),
        compiler_params=_params(("arbitrary", "arbitrary", "arbitrary")),
    )

    def fast(bound, q, k, v, *extra):
        return pl.pallas_call(fast_kern, in_specs=[smem] + qkv_specs + extra_specs, scratch_shapes=fast_scratch,
                              name=name + "_shifted", **common)((bound * LOG2E).reshape(1), q, k, v, *extra)

    def safe(bound, q, k, v, *extra):
        return pl.pallas_call(safe_kern, in_specs=qkv_specs + extra_specs, scratch_shapes=safe_scratch,
                              name=name, **common)(q, k, v, *extra)

    return lax.cond(bound <= MAX_SHIFT_BOUND, fast, safe, bound, q, k, v, *extra)


def _conv_module(prev_ref, cur_ref, next_ref, w_ref, b_ref, g_ref, beta_ref, ext_ref, cnv_ref, *, rows):
    i = pl.program_id(1)
    ni = pl.num_programs(1)
    tm = cur_ref.shape[1]
    halo = CONV_HALO
    prev = prev_ref[0]
    nxt = next_ref[0]
    ext_ref[0, 0:halo, :] = jnp.where(i > 0, prev, jnp.zeros_like(prev))
    ext_ref[0, halo:halo + tm, :] = cur_ref[0]
    ext_ref[0, halo + tm:2 * halo + tm, :] = jnp.where(i < ni - 1, nxt, jnp.zeros_like(nxt))
    off = halo - CONV_WIDTH // 2
    span = tm + 8 * ((off + CONV_WIDTH - 1) // 8)
    for s in range(1, 8):
        for c0 in range(0, span, rows):
            n = min(rows, span - c0)
            ext_ref[s, c0:c0 + n, :] = ext_ref[0, c0 + s:c0 + s + n, :]
    w = w_ref[...]
    bias = b_ref[...]
    g = g_ref[...]
    beta = beta_ref[...]

    for r0 in range(0, tm, rows):
        acc = jnp.zeros((rows, CONV_CH), F32)
        for j in range(CONV_WIDTH):
            a, s = divmod(j + off, 8)
            acc = acc + ext_ref[s, r0 + 8 * a:r0 + 8 * a + rows, :] * w[j:j + 1, :]
        acc = acc + bias
        mu = jnp.mean(acc, axis=-1, keepdims=True)
        d = acc - mu
        var = jnp.mean(d * d, axis=-1, keepdims=True)
        y = d * lax.rsqrt(var + EPS) * g + beta
        cnv_ref[r0:r0 + rows, :] = (y * jax.nn.sigmoid(y)).astype(cnv_ref.dtype)


def _out_kernel(x_ref, mod_ref, mla_ref, dif_ref, prev_ref, cur_ref, next_ref, wdw_ref, bdw_ref, gc_ref, bc_ref,
                w_ref, o_ref, ext_ref, cnv_ref, *, rows):
    n1 = mla_ref.shape[-1]
    n2 = n1 + dif_ref.shape[-1]
    acc = _dot(mla_ref[0], w_ref[0:n1, :]) + _dot(dif_ref[0], w_ref[n1:n2, :])
    _conv_module(prev_ref, cur_ref, next_ref, wdw_ref, bdw_ref, gc_ref, bc_ref, ext_ref, cnv_ref, rows=rows)
    acc = acc + _dot(cnv_ref[...], w_ref[n2:, :])
    o_ref[0] = x_ref[0] + mod_ref[0, 5:6, :] * acc


def _out_call(x, mod, o_mla, o_dif, glu, lw, *, tm, rows=64):
    B, S, D = x.shape
    C = glu.shape[-1]
    hb = tm // CONV_HALO
    nh = S // CONV_HALO
    tok = lambda w: pl.BlockSpec((1, tm, w), lambda b, i: (b, i, 0))
    const = lambda shape: pl.BlockSpec(shape, lambda b, i: (0, 0))
    return pl.pallas_call(
        functools.partial(_out_kernel, rows=min(rows, tm)),
        grid=(B, S // tm),
        in_specs=[tok(D), pl.BlockSpec((1, N_MOD, D), lambda b, i: (b, 0, 0)),
                  tok(o_mla.shape[-1]), tok(o_dif.shape[-1]),
                  pl.BlockSpec((1, CONV_HALO, C), lambda b, i: (b, jnp.maximum(i * hb - 1, 0), 0)),
                  tok(C),
                  pl.BlockSpec((1, CONV_HALO, C), lambda b, i: (b, jnp.minimum((i + 1) * hb, nh - 1), 0)),
                  const(lw["w_dw"].shape), const((1, C)), const((1, C)), const((1, C)),
                  const(lw["w_out"].shape)],
        out_specs=tok(D),
        out_shape=jax.ShapeDtypeStruct((B, S, D), F32),
        scratch_shapes=[pltpu.VMEM((8, tm + 2 * CONV_HALO, C), F32), pltpu.VMEM((tm, C), BF16)],
        compiler_params=_params(("arbitrary", "arbitrary")),
        name="out",
    )(x, mod, o_mla, o_dif, glu, glu, glu, lw["w_dw"], lw["b_dw"], lw["g_conv"], lw["b_conv"], lw["w_out"])


def _rope_tables(S):
    half = DIFF_DK // 2
    inv_freq = jnp.float32(ROPE_THETA) ** (-jnp.arange(half, dtype=F32) * (2.0 / DIFF_DK))
    ang = jnp.arange(S, dtype=jnp.int32).astype(F32)[:, None] * inv_freq[None, :]
    c, s, z = jnp.cos(ang), jnp.sin(ang), jnp.zeros_like(ang)
    return (jnp.concatenate([c, c, c, c], axis=1),
            jnp.concatenate([-s, z, -s, z], axis=1),
            jnp.concatenate([z, s, z, s], axis=1))


def _layer_weights(l, w_in, g_cq, g_ckv, w_uq, w_ukv, g_mla_q, g_mla_k, g_diff_q, g_diff_k,
                   lam_q1, lam_k1, lam_q2, lam_k2, g_diff_sub, w_dw, b_dw, g_conv, b_conv, w_out):
    wi = w_in[l]
    o_rope = MLA_Q_RANK + MLA_KV_RANK
    o_dq = o_rope + MLA_ROPE
    w_in_p = jnp.concatenate(
        [wi[:, :o_dq], jnp.zeros((D_MODEL, LANE - MLA_ROPE), wi.dtype), wi[:, o_dq:]], axis=1).astype(BF16)
    w_uq_p = jnp.pad(w_uq[l].reshape(MLA_Q_RANK, MLA_HEADS, MLA_QK),
                     ((0, 0), (0, 0), (0, MLA_QK_PAD - MLA_QK))).reshape(MLA_Q_RANK, -1).astype(BF16)
    pad_qk = lambda g: jnp.pad(g, (0, MLA_QK_PAD - MLA_QK)).reshape(1, MLA_QK_PAD)
    return dict(
        w_in=w_in_p, g_cq=g_cq[l].reshape(1, -1), g_ckv=g_ckv[l].reshape(1, -1),
        w_uq=w_uq_p, w_ukv=w_ukv[l].astype(BF16),
        g_q=pad_qk(g_mla_q[l] * (LOG2E / math.sqrt(MLA_QK))), g_k=pad_qk(g_mla_k[l]),
        g_dq=jnp.tile(g_diff_q[l] * (LOG2E / math.sqrt(DIFF_DK)), 2 * DIFF_HEADS).reshape(1, -1),
        g_dk=jnp.tile(g_diff_k[l], 2 * DIFF_HEADS).reshape(1, -1),
        bd=jnp.asarray(np.kron(np.eye(GROUP_TILE // DIFF_DK), np.ones((DIFF_DK, DIFF_DK))), BF16),
        lam=jnp.stack([lam_q1[l], lam_k1[l], lam_q2[l], lam_k2[l]]),
        b_mla=1.01 * math.sqrt(MLA_QK) * jnp.max(jnp.abs(g_mla_q[l])) * jnp.max(jnp.abs(g_mla_k[l])),
        b_dif=1.01 * math.sqrt(DIFF_DK) * jnp.max(jnp.abs(g_diff_q[l])) * jnp.max(jnp.abs(g_diff_k[l])),
        g_sub=g_diff_sub[l].reshape(1, -1),
        w_dw=jnp.pad(w_dw[l], ((0, 1), (0, 0))), b_dw=b_dw[l].reshape(1, -1),
        g_conv=g_conv[l].reshape(1, -1), b_conv=b_conv[l].reshape(1, -1),
        w_out=w_out[l].astype(BF16),
    )


def _tile(S, t):
    return min(S, t)


def kernel(x_prompt, x_sample, c_prompt, c_sample, w_ada, b_ada, g_norm, w_ffn1_in, w_ffn1_out, w_ffn2_in, w_ffn2_out, w_in, g_cq, g_ckv, w_uq, w_ukv, g_mla_q, g_mla_k, g_diff_q, g_diff_k, lam_q1, lam_k1, lam_q2, lam_k2, g_diff_sub, w_dw, b_dw, g_conv, b_conv, w_out):
    L = w_ada.shape[0]
    bp, bs = c_prompt.shape[0], c_sample.shape[0]
    rows = -(-(bp + bs) // 8) * 8
    c_all = jnp.pad(jnp.concatenate([c_prompt, c_sample], axis=0), ((0, rows - bp - bs), (0, 0)))
    mod_all = _ada_call(c_all, w_ada, b_ada)

    up1, dn1 = w_ffn1_in.astype(BF16), w_ffn1_out.astype(BF16)
    up2, dn2 = w_ffn2_in.astype(BF16), w_ffn2_out.astype(BF16)
    layers = [_layer_weights(l, w_in, g_cq, g_ckv, w_uq, w_ukv, g_mla_q, g_mla_k, g_diff_q, g_diff_k,
                             lam_q1, lam_k1, lam_q2, lam_k2, g_diff_sub, w_dw, b_dw, g_conv, b_conv, w_out)
              for l in range(L)]

    def trunk(x, row0):
        B, S, D = x.shape
        tm, tm_ffn, tq, tk = (_tile(S, t) for t in (TOKEN_TILE, FFN_TOKEN_TILE, ATTN_Q_TILE, ATTN_K_TILE))
        tabs = _rope_tables(S)
        for l in range(L):
            lw = layers[l]
            mod = mod_all[l, row0:row0 + B].reshape(B, N_MOD, D)
            gn = g_norm[l]
            x = _ffn_call(x, mod, gn[0:1], up1, dn1, gn[3:4], layer=l, sub=0, final_norm=False, tm=tm_ffn)
            q, k, v, dq, dk, dv, glu = _proj_call(x, mod, gn[1:2], lw, tabs, tm=tm)
            o_mla = _attn_call(q, k, v, (), lw["b_mla"], tq=tq, tk=tk)
            lam_init = 0.8 - 0.6 * math.exp(-0.3 * l)
            o_dif = _attn_call(dq, dk, dv, (lw["lam"], lw["g_sub"]), lw["b_dif"], tq=tq, tk=tk, lam_init=lam_init)
            x = _out_call(x, mod, o_mla, o_dif, glu, lw, tm=tm)
            x = _ffn_call(x, mod, gn[2:3], up2, dn2, gn[3:4], layer=l, sub=2, final_norm=True, tm=tm_ffn)
        return x

    return (trunk(x_prompt, 0), trunk(x_sample, bp))
```
